```python
import jax, jax.numpy as jnp
from jax import lax
import numpy as np

D_MODEL = 1024
BATCH = 32
SEQ = 2048
DEPTH = 2

D_CONV = D_MODEL
CONV_GROUPS = 16
CONV_WIDTH = 3
D_SGU = D_MODEL
SGU_GROUPS = 8
SGU_GROUP_DIM = D_SGU // SGU_GROUPS
CHUNK = 128
D_FF = 2816
N_BRANCH = 2
EPS = 1e-6

IN_COLS = 3 * D_CONV + 2 * D_SGU + N_BRANCH * D_MODEL

kernel_name = "hybrid_shortconv_sgu_gated_merge"


def rmsnorm(x, g):
    xf = x.astype(jnp.float32)
    y = xf * lax.rsqrt(jnp.mean(xf * xf, axis=-1, keepdims=True) + EPS)
    return (y * g.astype(jnp.float32)).astype(x.dtype)


def layernorm(x, g, b):
    xf = x.astype(jnp.float32)
    mu = jnp.mean(xf, axis=-1, keepdims=True)
    xc = xf - mu
    var = jnp.mean(xc * xc, axis=-1, keepdims=True)
    y = xc * lax.rsqrt(var + EPS)
    return (y * g.astype(jnp.float32) + b.astype(jnp.float32)).astype(x.dtype)


def causal_dwconv3(x, w):
    s = x.shape[1]
    xp = jnp.pad(x, ((0, 0), (CONV_WIDTH - 1, 0), (0, 0)))
    return xp[:, :s] * w[0] + xp[:, 1:s + 1] * w[1] + xp[:, 2:s + 2] * w[2]


def short_conv_mixer(b_gate, c_gate, xin, conv_w):
    return b_gate * causal_dwconv3(c_gate * xin, conv_w)


def spatial_gating_mixer(u, v, ln_g, ln_b, w_s, b_s):
    bsz, s, _ = v.shape
    n_chunks = s // CHUNK
    vn = layernorm(v, ln_g, ln_b).reshape(bsz, n_chunks, CHUNK, SGU_GROUPS, SGU_GROUP_DIM)
    mask = jnp.tril(jnp.ones((CHUNK, CHUNK), dtype=bool))
    w = jnp.where(mask[None], w_s, jnp.zeros((), w_s.dtype))
    mixed = jnp.einsum('gts,bnsgc->bntgc', w, vn)
    mixed = mixed + jnp.swapaxes(b_s, 0, 1)[None, None, :, :, None]
    return u * mixed.reshape(bsz, s, D_SGU)


def conv_gated_mlp(h, w_up, conv_w, w_down):
    up = causal_dwconv3(h @ w_up, conv_w)
    gate, val = jnp.split(up, 2, axis=-1)
    return (jax.nn.silu(gate) * val) @ w_down


def setup_inputs(seed: int = 0) -> dict:
    key = jax.random.key(seed)
    ks = jax.random.split(key, 16)
    f32 = jnp.float32

    def nrm(k, shape, scale):
        return jax.random.normal(k, shape, f32) * scale

    x = jax.random.normal(ks[0], (BATCH, SEQ, D_MODEL), f32)
    mix_norm_g = 1.0 + nrm(ks[1], (DEPTH, D_MODEL), 0.02)
    w_in = nrm(ks[2], (DEPTH, D_MODEL, IN_COLS), D_MODEL ** -0.5)
    conv_a_w = nrm(ks[3], (DEPTH, CONV_WIDTH, D_CONV), CONV_WIDTH ** -0.5)
    ln_v_g = 1.0 + nrm(ks[4], (DEPTH, D_SGU), 0.02)
    ln_v_b = nrm(ks[5], (DEPTH, D_SGU), 0.02)
    w_s = nrm(ks[6], (DEPTH, SGU_GROUPS, CHUNK, CHUNK), CHUNK ** -0.5)
    b_s = 1.0 + nrm(ks[7], (DEPTH, SGU_GROUPS, CHUNK), 0.02)
    w_out = nrm(ks[8], (DEPTH, D_MODEL, D_MODEL), D_MODEL ** -0.5)
    ffn_norm_g = 1.0 + nrm(ks[9], (DEPTH, D_MODEL), 0.02)
    w_up = nrm(ks[10], (DEPTH, D_MODEL, 2 * D_FF), D_MODEL ** -0.5)
    conv_ffn_w = nrm(ks[11], (DEPTH, CONV_WIDTH, 2 * D_FF), CONV_WIDTH ** -0.5)
    w_down = nrm(ks[12], (DEPTH, D_FF, D_MODEL), D_FF ** -0.5)
    final_norm_g = 1.0 + nrm(ks[13], (D_MODEL,), 0.02)
    return {"x": x, "mix_norm_g": mix_norm_g, "w_in": w_in, "conv_a_w": conv_a_w,
            "ln_v_g": ln_v_g, "ln_v_b": ln_v_b, "w_s": w_s, "b_s": b_s,
            "w_out": w_out, "ffn_norm_g": ffn_norm_g, "w_up": w_up,
            "conv_ffn_w": conv_ffn_w, "w_down": w_down, "final_norm_g": final_norm_g}


def reference(x, mix_norm_g, w_in, conv_a_w, ln_v_g, ln_v_b, w_s, b_s, w_out,
              ffn_norm_g, w_up, conv_ffn_w, w_down, final_norm_g):
    split_pts = [D_CONV, 2 * D_CONV, 3 * D_CONV, 3 * D_CONV + D_SGU,
                 3 * D_CONV + 2 * D_SGU, 3 * D_CONV + 2 * D_SGU + D_MODEL]
    for l in range(DEPTH):
        h = rmsnorm(x, mix_norm_g[l])
        proj = h @ w_in[l]
        b_gate, c_gate, xin, u, v, g_a, g_b = jnp.split(proj, split_pts, axis=-1)
        y_a = short_conv_mixer(b_gate, c_gate, xin, conv_a_w[l])
        y_b = spatial_gating_mixer(u, v, ln_v_g[l], ln_v_b[l], w_s[l], b_s[l])
        merged = jax.nn.sigmoid(g_a) * y_a + jax.nn.sigmoid(g_b) * y_b
        x = x + merged @ w_out[l]
        h = rmsnorm(x, ffn_norm_g[l])
        x = x + conv_gated_mlp(h, w_up[l], conv_ffn_w[l], w_down[l])
    return rmsnorm(x, final_norm_g)
```

```python
import functools

import jax
import jax.numpy as jnp
from jax import lax
from jax.experimental import pallas as pl
from jax.experimental.pallas import tpu as pltpu

EPS = 1e-6
CHUNK = 128
SGU_GROUPS = 8
CONV_WIDTH = 3
SUBLANES = 8
TILE_ROWS = 512
VMEM_LIMIT_BYTES = 56 * 1024 * 1024

_BF16 = jnp.bfloat16
_F32 = jnp.float32


def _rmsnorm(x, g):
    ms = jnp.mean(x * x, axis=-1, keepdims=True)
    return x * lax.rsqrt(ms + EPS) * g


def _dot(a, b):
    return jnp.dot(a, b, preferred_element_type=_F32)


def _load_history(zbuf_ref, rows, first_tile):
    prev = zbuf_ref[rows:rows + SUBLANES, :]
    zbuf_ref[0:SUBLANES, :] = jnp.where(first_tile, jnp.zeros_like(prev), prev)


def _conv3(zbuf_ref, w_ref, rows, cols):
    w = w_ref[:, cols]
    z2 = zbuf_ref[SUBLANES - 2:SUBLANES - 2 + rows, cols]
    z1 = zbuf_ref[SUBLANES - 1:SUBLANES - 1 + rows, cols]
    z0 = zbuf_ref[SUBLANES:SUBLANES + rows, cols]
    return z2 * w[0:1, :] + z1 * w[1:2, :] + z0 * w[2:3, :]


def _mixer_kernel(x_ref, g_ref, win_ref, cw_ref, lng_ref, lnb_ref, ws_ref, bias_ref, wout_ref,
                  o_ref, h_ref, zbuf_ref, vn_ref, mixed_ref, merged_ref):
    rows, d = x_ref.shape
    first_tile = pl.program_id(1) == 0
    col = lambda k: slice(k * d, (k + 1) * d)
    full = slice(0, d)

    x = x_ref[...]
    h_ref[...] = _rmsnorm(x, g_ref[...]).astype(_BF16)
    proj = lambda k: _dot(h_ref[...], win_ref[:, col(k)])

    _load_history(zbuf_ref, rows, first_tile)
    zbuf_ref[SUBLANES:SUBLANES + rows, :] = proj(1) * proj(2)
    y_a = proj(0) * _conv3(zbuf_ref, cw_ref, rows, full)
    merged_ref[...] = jax.nn.sigmoid(proj(5)) * y_a

    v = proj(4)
    mu = jnp.mean(v, axis=-1, keepdims=True)
    vc = v - mu
    var = jnp.mean(vc * vc, axis=-1, keepdims=True)
    vn_ref[...] = (vc * lax.rsqrt(var + EPS) * lng_ref[...] + lnb_ref[...]).astype(_BF16)

    t_idx = lax.broadcasted_iota(jnp.int32, (CHUNK, CHUNK), 0)
    s_idx = lax.broadcasted_iota(jnp.int32, (CHUNK, CHUNK), 1)
    causal = s_idx <= t_idx
    gdim = d // SGU_GROUPS
    n_chunks = rows // CHUNK
    for g in range(SGU_GROUPS):
        wg = jnp.where(causal, ws_ref[g], 0.0).astype(_BF16)
        gcols = slice(g * gdim, (g + 1) * gdim)
        for n in range(0, n_chunks, 2):
            r0 = slice(n * CHUNK, (n + 1) * CHUNK)
            r1 = slice((n + 1) * CHUNK, (n + 2) * CHUNK)
            rhs = jnp.concatenate([vn_ref[r0, gcols], vn_ref[r1, gcols]], axis=1)
            res = _dot(wg, rhs)
            mixed_ref[r0, gcols] = res[:, :gdim]
            mixed_ref[r1, gcols] = res[:, gdim:]
    bias = bias_ref[...]
    for n in range(n_chunks):
        r = slice(n * CHUNK, (n + 1) * CHUNK)
        mixed_ref[r, :] = mixed_ref[r, :] + bias
    y_b = proj(3) * mixed_ref[...]
    merged_ref[...] = merged_ref[...] + jax.nn.sigmoid(proj(6)) * y_b

    o_ref[...] = x + _dot(merged_ref[...].astype(_BF16), wout_ref[...])


def _ffn_kernel(x_ref, g_ref, wup_ref, cw_ref, wdown_ref, fg_ref, o_ref,
                h_ref, zbuf_ref, act_ref, *, final_norm, col_block):
    rows, d = x_ref.shape
    d_ff = wdown_ref.shape[0]
    first_tile = pl.program_id(1) == 0

    x = x_ref[...]
    h_ref[...] = _rmsnorm(x, g_ref[...]).astype(_BF16)
    _load_history(zbuf_ref, rows, first_tile)
    for j in range(d_ff // col_block):
        gcols = slice(j * col_block, (j + 1) * col_block)
        vcols = slice(d_ff + j * col_block, d_ff + (j + 1) * col_block)
        zbuf_ref[SUBLANES:SUBLANES + rows, gcols] = _dot(h_ref[...], wup_ref[:, gcols])
        zbuf_ref[SUBLANES:SUBLANES + rows, vcols] = _dot(h_ref[...], wup_ref[:, vcols])
        gate = _conv3(zbuf_ref, cw_ref, rows, gcols)
        val = _conv3(zbuf_ref, cw_ref, rows, vcols)
        act_ref[:, gcols] = (gate * jax.nn.sigmoid(gate) * val).astype(_BF16)
    y = x + _dot(act_ref[...], wdown_ref[...])
    if final_norm:
        y = _rmsnorm(y, fg_ref[...])
    o_ref[...] = y


def _resident(shape):
    zeros = (0,) * len(shape)
    return pl.BlockSpec(shape, lambda b, s: zeros, pipeline_mode=pl.Buffered(1))


def _mixer_call(x, g, w_in, conv_w, ln_g, ln_b, w_s, bias, w_out):
    bsz, seq, d = x.shape
    rows = TILE_ROWS
    tile = pl.BlockSpec((None, rows, d), lambda b, s: (b, s, 0))
    return pl.pallas_call(
        _mixer_kernel,
        grid=(bsz, seq // rows),
        in_specs=[tile, _resident(g.shape), _resident(w_in.shape), _resident(conv_w.shape),
                  _resident(ln_g.shape), _resident(ln_b.shape), _resident(w_s.shape),
                  _resident(bias.shape), _resident(w_out.shape)],
        out_specs=tile,
        out_shape=jax.ShapeDtypeStruct(x.shape, x.dtype),
        scratch_shapes=[
            pltpu.VMEM((rows, d), _BF16),
            pltpu.VMEM((rows + SUBLANES, d), _F32),
            pltpu.VMEM((rows, d), _BF16),
            pltpu.VMEM((rows, d), _F32),
            pltpu.VMEM((rows, d), _F32),
        ],
        compiler_params=pltpu.CompilerParams(
            dimension_semantics=("arbitrary", "arbitrary"),
            vmem_limit_bytes=VMEM_LIMIT_BYTES),
        name="token_mixer",
    )(x, g, w_in, conv_w, ln_g, ln_b, w_s, bias, w_out)


def _ffn_call(x, g, w_up, conv_w, w_down, final_g, *, final_norm):
    bsz, seq, d = x.shape
    d_ff = w_down.shape[0]
    rows = TILE_ROWS
    tile = pl.BlockSpec((None, rows, d), lambda b, s: (b, s, 0))
    body = functools.partial(_ffn_kernel, final_norm=final_norm, col_block=256)
    return pl.pallas_call(
        body,
        grid=(bsz, seq // rows),
        in_specs=[tile, _resident(g.shape), _resident(w_up.shape), _resident(conv_w.shape),
                  _resident(w_down.shape), _resident(final_g.shape)],
        out_specs=tile,
        out_shape=jax.ShapeDtypeStruct(x.shape, x.dtype),
        scratch_shapes=[
            pltpu.VMEM((rows, d), _BF16),
            pltpu.VMEM((rows + SUBLANES, 2 * d_ff), _F32),
            pltpu.VMEM((rows, d_ff), _BF16),
        ],
        compiler_params=pltpu.CompilerParams(
            dimension_semantics=("arbitrary", "arbitrary"),
            vmem_limit_bytes=VMEM_LIMIT_BYTES),
        name="channel_mixer",
    )(x, g, w_up, conv_w, w_down, final_g)


def kernel(x, mix_norm_g, w_in, conv_a_w, ln_v_g, ln_v_b, w_s, b_s, w_out, ffn_norm_g, w_up,
           conv_ffn_w, w_down, final_norm_g):
    depth, d = mix_norm_g.shape
    assert x.shape[1] % TILE_ROWS == 0 and TILE_ROWS % (2 * CHUNK) == 0
    gdim = d // SGU_GROUPS
    row = lambda a: a.reshape(1, -1)
    bias = jnp.repeat(jnp.swapaxes(b_s, 1, 2), gdim, axis=-1)
    for l in range(depth):
        x = _mixer_call(x, row(mix_norm_g[l]), w_in[l].astype(_BF16), conv_a_w[l],
                        row(ln_v_g[l]), row(ln_v_b[l]), w_s[l], bias[l], w_out[l].astype(_BF16))
        x = _ffn_call(x, row(ffn_norm_g[l]), w_up[l].astype(_BF16), conv_ffn_w[l],
                      w_down[l].astype(_BF16), row(final_norm_g), final_norm=(l == depth - 1))
    return x
```

```python
import functools

import jax
import jax.numpy as jnp
from jax import lax
from jax.experimental import pallas as pl
from jax.experimental.pallas import tpu as pltpu

EPS = 1e-6
CHUNK = 128
SGU_GROUPS = 8
CONV_WIDTH = 3
SUBLANES = 8
LANES = 128
TILE_ROWS = 512
VMEM_LIMIT_BYTES = 56 * 1024 * 1024

_BF16 = jnp.bfloat16
_F32 = jnp.float32


def _rmsnorm(x, g):
    ms = jnp.mean(x * x, axis=-1, keepdims=True)
    return x * lax.rsqrt(ms + EPS) * g


def _dot(a, b):
    return jnp.dot(a, b, preferred_element_type=_F32)


def _load_history(zbuf_ref, rows, first_tile):
    prev = zbuf_ref[:, rows:rows + SUBLANES, :]
    zbuf_ref[:, 0:SUBLANES, :] = jnp.where(first_tile, jnp.zeros_like(prev), prev)


def _store_slabs(zbuf_ref, first_slab, rows, value):
    for k in range(value.shape[1] // LANES):
        zbuf_ref[first_slab + k, SUBLANES:SUBLANES + rows, :] = value[:, k * LANES:(k + 1) * LANES]


def _conv3(zbuf_ref, w_ref, rows, slab):
    w = w_ref[:, slab * LANES:(slab + 1) * LANES]
    z2 = zbuf_ref[slab, SUBLANES - 2:SUBLANES - 2 + rows, :]
    z1 = zbuf_ref[slab, SUBLANES - 1:SUBLANES - 1 + rows, :]
    z0 = zbuf_ref[slab, SUBLANES:SUBLANES + rows, :]
    return z2 * w[0:1, :] + z1 * w[1:2, :] + z0 * w[2:3, :]


def _mixer_kernel(x_ref, g_ref, win_ref, cw_ref, lng_ref, lnb_ref, ws_ref, bias_ref, wout_ref,
                  o_ref, h_ref, zbuf_ref, vn_ref, mixed_ref, merged_ref):
    rows, d = x_ref.shape
    first_tile = pl.program_id(1) == 0
    col = lambda k: slice(k * d, (k + 1) * d)

    x = x_ref[...]
    h_ref[...] = _rmsnorm(x, g_ref[...]).astype(_BF16)
    proj = lambda k: _dot(h_ref[...], win_ref[:, col(k)])

    _load_history(zbuf_ref, rows, first_tile)
    _store_slabs(zbuf_ref, 0, rows, proj(1) * proj(2))
    conv = jnp.concatenate([_conv3(zbuf_ref, cw_ref, rows, s) for s in range(d // LANES)], axis=1)
    y_a = proj(0) * conv
    merged_ref[...] = jax.nn.sigmoid(proj(5)) * y_a

    v = proj(4)
    mu = jnp.mean(v, axis=-1, keepdims=True)
    vc = v - mu
    var = jnp.mean(vc * vc, axis=-1, keepdims=True)
    vn_ref[...] = (vc * lax.rsqrt(var + EPS) * lng_ref[...] + lnb_ref[...]).astype(_BF16)

    t_idx = lax.broadcasted_iota(jnp.int32, (CHUNK, CHUNK), 0)
    s_idx = lax.broadcasted_iota(jnp.int32, (CHUNK, CHUNK), 1)
    causal = s_idx <= t_idx
    gdim = d // SGU_GROUPS
    n_chunks = rows // CHUNK
    for g in range(SGU_GROUPS):
        wg = jnp.where(causal, ws_ref[g], 0.0).astype(_BF16)
        gcols = slice(g * gdim, (g + 1) * gdim)
        for n in range(0, n_chunks, 2):
            r0 = slice(n * CHUNK, (n + 1) * CHUNK)
            r1 = slice((n + 1) * CHUNK, (n + 2) * CHUNK)
            rhs = jnp.concatenate([vn_ref[r0, gcols], vn_ref[r1, gcols]], axis=1)
            res = _dot(wg, rhs)
            mixed_ref[r0, gcols] = res[:, :gdim]
            mixed_ref[r1, gcols] = res[:, gdim:]
    bias = bias_ref[...]
    for n in range(n_chunks):
        r = slice(n * CHUNK, (n + 1) * CHUNK)
        mixed_ref[r, :] = mixed_ref[r, :] + bias
    y_b = proj(3) * mixed_ref[...]
    merged_ref[...] = merged_ref[...] + jax.nn.sigmoid(proj(6)) * y_b

    o_ref[...] = x + _dot(merged_ref[...].astype(_BF16), wout_ref[...])


def _ffn_kernel(x_ref, g_ref, wup_ref, cw_ref, wdown_ref, fg_ref, o_ref,
                h_ref, zbuf_ref, act_ref, *, final_norm, col_block):
    rows, d = x_ref.shape
    d_ff = wdown_ref.shape[0]
    first_tile = pl.program_id(1) == 0
    slabs_per_block = col_block // LANES

    x = x_ref[...]
    h_ref[...] = _rmsnorm(x, g_ref[...]).astype(_BF16)
    _load_history(zbuf_ref, rows, first_tile)
    for j in range(d_ff // col_block):
        for c0 in (j * col_block, d_ff + j * col_block):
            _store_slabs(zbuf_ref, c0 // LANES, rows, _dot(h_ref[...], wup_ref[:, c0:c0 + col_block]))
        for k in range(slabs_per_block):
            s = j * slabs_per_block + k
            gate = _conv3(zbuf_ref, cw_ref, rows, s)
            val = _conv3(zbuf_ref, cw_ref, rows, d_ff // LANES + s)
            act_ref[:, s * LANES:(s + 1) * LANES] = (gate * jax.nn.sigmoid(gate) * val).astype(_BF16)
    y = x + _dot(act_ref[...], wdown_ref[...])
    if final_norm:
        y = _rmsnorm(y, fg_ref[...])
    o_ref[...] = y


def _resident(shape):
    zeros = (0,) * len(shape)
    return pl.BlockSpec(shape, lambda b, s: zeros, pipeline_mode=pl.Buffered(1))


def _mixer_call(x, g, w_in, conv_w, ln_g, ln_b, w_s, bias, w_out):
    bsz, seq, d = x.shape
    rows = TILE_ROWS
    tile = pl.BlockSpec((None, rows, d), lambda b, s: (b, s, 0))
    return pl.pallas_call(
        _mixer_kernel,
        grid=(bsz, seq // rows),
        in_specs=[tile, _resident(g.shape), _resident(w_in.shape), _resident(conv_w.shape),
                  _resident(ln_g.shape), _resident(ln_b.shape), _resident(w_s.shape),
                  _resident(bias.shape), _resident(w_out.shape)],
        out_specs=tile,
        out_shape=jax.ShapeDtypeStruct(x.shape, x.dtype),
        scratch_shapes=[
            pltpu.VMEM((rows, d), _BF16),
            pltpu.VMEM((d // LANES, rows + SUBLANES, LANES), _F32),
            pltpu.VMEM((rows, d), _BF16),
            pltpu.VMEM((rows, d), _F32),
            pltpu.VMEM((rows, d), _F32),
        ],
        compiler_params=pltpu.CompilerParams(
            dimension_semantics=("arbitrary", "arbitrary"),
            vmem_limit_bytes=VMEM_LIMIT_BYTES),
        name="token_mixer",
    )(x, g, w_in, conv_w, ln_g, ln_b, w_s, bias, w_out)


def _ffn_call(x, g, w_up, conv_w, w_down, final_g, *, final_norm):
    bsz, seq, d = x.shape
    d_ff = w_down.shape[0]
    rows = TILE_ROWS
    tile = pl.BlockSpec((None, rows, d), lambda b, s: (b, s, 0))
    body = functools.partial(_ffn_kernel, final_norm=final_norm, col_block=256)
    return pl.pallas_call(
        body,
        grid=(bsz, seq // rows),
        in_specs=[tile, _resident(g.shape), _resident(w_up.shape), _resident(conv_w.shape),
                  _resident(w_down.shape), _resident(final_g.shape)],
        out_specs=tile,
        out_shape=jax.ShapeDtypeStruct(x.shape, x.dtype),
        scratch_shapes=[
            pltpu.VMEM((rows, d), _BF16),
            pltpu.VMEM((2 * d_ff // LANES, rows + SUBLANES, LANES), _F32),
            pltpu.VMEM((rows, d_ff), _BF16),
        ],
        compiler_params=pltpu.CompilerParams(
            dimension_semantics=("arbitrary", "arbitrary"),
            vmem_limit_bytes=VMEM_LIMIT_BYTES),
        name="channel_mixer",
    )(x, g, w_up, conv_w, w_down, final_g)


def kernel(x, mix_norm_g, w_in, conv_a_w, ln_v_g, ln_v_b, w_s, b_s, w_out, ffn_norm_g, w_up,
           conv_ffn_w, w_down, final_norm_g):
    depth, d = mix_norm_g.shape
    assert x.shape[1] % TILE_ROWS == 0 and TILE_ROWS % (2 * CHUNK) == 0
    gdim = d // SGU_GROUPS
    row = lambda a: a.reshape(1, -1)
    bias = jnp.repeat(jnp.swapaxes(b_s, 1, 2), gdim, axis=-1)
    for l in range(depth):
        x = _mixer_call(x, row(mix_norm_g[l]), w_in[l].astype(_BF16), conv_a_w[l],
                        row(ln_v_g[l]), row(ln_v_b[l]), w_s[l], bias[l], w_out[l].astype(_BF16))
        x = _ffn_call(x, row(ffn_norm_g[l]), w_up[l].astype(_BF16), conv_ffn_w[l],
                      w_down[l].astype(_BF16), row(final_norm_g), final_norm=(l == depth - 1))
    return x
```

```python
import functools

import jax
import jax.numpy as jnp
from jax import lax
from jax.experimental import pallas as pl
from jax.experimental.pallas import tpu as pltpu

EPS = 1e-6
CHUNK = 128
SGU_GROUPS = 8
CONV_WIDTH = 3
SUBLANES = 8
LANES = 128
TILE_ROWS = 512
VMEM_LIMIT_BYTES = 56 * 1024 * 1024

_BF16 = jnp.bfloat16
_F32 = jnp.float32


def _rmsnorm(x, g):
    ms = jnp.mean(x * x, axis=-1, keepdims=True)
    return x * lax.rsqrt(ms + EPS) * g


def _dot(a, b):
    return jnp.dot(a, b, preferred_element_type=_F32)


def _store_slabs(zbuf_ref, first_slab, row0, rows, value):
    for k in range(value.shape[1] // LANES):
        zbuf_ref[first_slab + k, row0:row0 + rows, :] = value[:, k * LANES:(k + 1) * LANES]


def _conv3_offset(zbuf_ref, w_ref, rows, slab):
    w = w_ref[:, slab * LANES:(slab + 1) * LANES]
    z2 = zbuf_ref[slab, SUBLANES - 2:SUBLANES - 2 + rows, :]
    z1 = zbuf_ref[slab, SUBLANES - 1:SUBLANES - 1 + rows, :]
    z0 = zbuf_ref[slab, SUBLANES:SUBLANES + rows, :]
    return z2 * w[0:1, :] + z1 * w[1:2, :] + z0 * w[2:3, :]


def _mixer_kernel(x_ref, g_ref, win_ref, cw_ref, lng_ref, lnb_ref, ws_ref, bias_ref, wout_ref,
                  o_ref, h_ref, zbuf_ref, vn_ref, mixed_ref, merged_ref):
    rows, d = x_ref.shape
    first_tile = pl.program_id(1) == 0
    col = lambda k: slice(k * d, (k + 1) * d)

    x = x_ref[...]
    h_ref[...] = _rmsnorm(x, g_ref[...]).astype(_BF16)
    proj = lambda k: _dot(h_ref[...], win_ref[:, col(k)])

    prev = zbuf_ref[:, rows:rows + SUBLANES, :]
    zbuf_ref[:, 0:SUBLANES, :] = jnp.where(first_tile, jnp.zeros_like(prev), prev)
    _store_slabs(zbuf_ref, 0, SUBLANES, rows, proj(1) * proj(2))
    conv = jnp.concatenate([_conv3_offset(zbuf_ref, cw_ref, rows, s) for s in range(d // LANES)],
                           axis=1)
    y_a = proj(0) * conv
    merged_ref[...] = jax.nn.sigmoid(proj(5)) * y_a

    v = proj(4)
    mu = jnp.mean(v, axis=-1, keepdims=True)
    vc = v - mu
    var = jnp.mean(vc * vc, axis=-1, keepdims=True)
    vn_ref[...] = (vc * lax.rsqrt(var + EPS) * lng_ref[...] + lnb_ref[...]).astype(_BF16)

    t_idx = lax.broadcasted_iota(jnp.int32, (CHUNK, CHUNK), 0)
    s_idx = lax.broadcasted_iota(jnp.int32, (CHUNK, CHUNK), 1)
    causal = s_idx <= t_idx
    gdim = d // SGU_GROUPS
    n_chunks = rows // CHUNK
    for g in range(SGU_GROUPS):
        wg = jnp.where(causal, ws_ref[g], 0.0).astype(_BF16)
        gcols = slice(g * gdim, (g + 1) * gdim)
        for n in range(0, n_chunks, 2):
            r0 = slice(n * CHUNK, (n + 1) * CHUNK)
            r1 = slice((n + 1) * CHUNK, (n + 2) * CHUNK)
            rhs = jnp.concatenate([vn_ref[r0, gcols], vn_ref[r1, gcols]], axis=1)
            res = _dot(wg, rhs)
            mixed_ref[r0, gcols] = res[:, :gdim]
            mixed_ref[r1, gcols] = res[:, gdim:]
    bias = bias_ref[...]
    for n in range(n_chunks):
        r = slice(n * CHUNK, (n + 1) * CHUNK)
        mixed_ref[r, :] = mixed_ref[r, :] + bias
    y_b = proj(3) * mixed_ref[...]
    merged_ref[...] = merged_ref[...] + jax.nn.sigmoid(proj(6)) * y_b

    o_ref[...] = x + _dot(merged_ref[...].astype(_BF16), wout_ref[...])


HIST = 2 * SUBLANES


def _tile_copies(hbm_ref, buf_ref, sem_ref, tile, slot, to_hbm):
    r = buf_ref.shape[1]
    copies = []
    for sub in range(SUBLANES):
        hbm = hbm_ref.at[pl.ds((tile * SUBLANES + sub) * r, r), :]
        vmem = buf_ref.at[slot, :, sub, :]
        src, dst = (vmem, hbm) if to_hbm else (hbm, vmem)
        copies.append(pltpu.make_async_copy(src, dst, sem_ref.at[slot, sub]))
    return copies


def _conv3_groups(zbuf_ref, w_ref, rows, slab):
    w = w_ref[:, slab * LANES:(slab + 1) * LANES]
    z2 = zbuf_ref[slab, 0:rows, :]
    z1 = zbuf_ref[slab, SUBLANES:SUBLANES + rows, :]
    z0 = zbuf_ref[slab, HIST:HIST + rows, :]
    return z2 * w[0:1, :] + z1 * w[1:2, :] + z0 * w[2:3, :]


def _fill_history(zbuf_ref, pbuf_ref, rows, slab):
    last = zbuf_ref[slab, rows:rows + HIST, :]
    prev = pbuf_ref[slab]
    is_last_sublane = lax.broadcasted_iota(jnp.int32, (SUBLANES, LANES), 0) == SUBLANES - 1
    for k in range(2):
        rk = slice(k * SUBLANES, (k + 1) * SUBLANES)
        src = jnp.where(is_last_sublane, prev[rk], last[rk])
        zbuf_ref[slab, rk, :] = pltpu.roll(src, 1, 0)


def _ffn_kernel(x_hbm, g_ref, wup_ref, cw_ref, wdown_ref, fg_ref, o_hbm,
                xbuf, obuf, in_sem, out_sem, h_ref, zbuf_ref, pbuf_ref, act_ref,
                *, final_norm, col_block, tiles_per_seq):
    n_r, _, d = xbuf.shape[1:]
    rows = n_r * SUBLANES
    d_ff = wdown_ref.shape[0]
    slabs_per_block = col_block // LANES
    t = pl.program_id(0)
    n_tiles = pl.num_programs(0)
    slot = lax.rem(t, 2)
    starts_seq = lax.rem(t, tiles_per_seq) == 0

    @pl.when(t == 0)
    def _():
        zbuf_ref[:, rows:rows + HIST, :] = jnp.zeros((zbuf_ref.shape[0], HIST, LANES), _F32)
        for c in _tile_copies(x_hbm, xbuf, in_sem, 0, 0, to_hbm=False):
            c.start()

    @pl.when(t + 1 < n_tiles)
    def _():
        for c in _tile_copies(x_hbm, xbuf, in_sem, t + 1, 1 - slot, to_hbm=False):
            c.start()

    for c in _tile_copies(x_hbm, xbuf, in_sem, t, slot, to_hbm=False):
        c.wait()

    @pl.when(t >= 2)
    def _():
        for c in _tile_copies(o_hbm, obuf, out_sem, t - 2, slot, to_hbm=True):
            c.wait()

    x = xbuf[slot].reshape(rows, d)
    h_ref[...] = _rmsnorm(x, g_ref[...]).astype(_BF16)
    prev = zbuf_ref[:, rows:rows + HIST, :]
    pbuf_ref[...] = jnp.where(starts_seq, jnp.zeros_like(prev), prev)
    for j in range(d_ff // col_block):
        for c0 in (j * col_block, d_ff + j * col_block):
            up = _dot(h_ref[...], wup_ref[:, c0:c0 + col_block])
            _store_slabs(zbuf_ref, c0 // LANES, HIST, rows, up)
            for k in range(slabs_per_block):
                _fill_history(zbuf_ref, pbuf_ref, rows, c0 // LANES + k)
        for k in range(slabs_per_block):
            s = j * slabs_per_block + k
            gate = _conv3_groups(zbuf_ref, cw_ref, rows, s)
            val = _conv3_groups(zbuf_ref, cw_ref, rows, d_ff // LANES + s)
            act_ref[:, s * LANES:(s + 1) * LANES] = (gate * jax.nn.sigmoid(gate) * val).astype(_BF16)
    y = x + _dot(act_ref[...], wdown_ref[...])
    if final_norm:
        y = _rmsnorm(y, fg_ref[...])
    obuf[slot] = y.reshape(n_r, SUBLANES, d)

    for c in _tile_copies(o_hbm, obuf, out_sem, t, slot, to_hbm=True):
        c.start()

    @pl.when(t == n_tiles - 1)
    def _():
        for c in _tile_copies(o_hbm, obuf, out_sem, t, slot, to_hbm=True):
            c.wait()

    @pl.when(jnp.logical_and(t == n_tiles - 1, t >= 1))
    def _():
        for c in _tile_copies(o_hbm, obuf, out_sem, t - 1, 1 - slot, to_hbm=True):
            c.wait()


def _resident(shape, n_grid_axes):
    zeros = (0,) * len(shape)
    return pl.BlockSpec(shape, lambda *_: zeros, pipeline_mode=pl.Buffered(1))


def _mixer_call(x, g, w_in, conv_w, ln_g, ln_b, w_s, bias, w_out):
    bsz, seq, d = x.shape
    rows = TILE_ROWS
    tile = pl.BlockSpec((None, rows, d), lambda b, s: (b, s, 0))
    res = functools.partial(_resident, n_grid_axes=2)
    return pl.pallas_call(
        _mixer_kernel,
        grid=(bsz, seq // rows),
        in_specs=[tile, res(g.shape), res(w_in.shape), res(conv_w.shape), res(ln_g.shape),
                  res(ln_b.shape), res(w_s.shape), res(bias.shape), res(w_out.shape)],
        out_specs=tile,
        out_shape=jax.ShapeDtypeStruct(x.shape, x.dtype),
        scratch_shapes=[
            pltpu.VMEM((rows, d), _BF16),
            pltpu.VMEM((d // LANES, SUBLANES + rows, LANES), _F32),
            pltpu.VMEM((rows, d), _BF16),
            pltpu.VMEM((rows, d), _F32),
            pltpu.VMEM((rows, d), _F32),
        ],
        compiler_params=pltpu.CompilerParams(
            dimension_semantics=("arbitrary", "arbitrary"),
            vmem_limit_bytes=VMEM_LIMIT_BYTES),
        name="token_mixer",
    )(x, g, w_in, conv_w, ln_g, ln_b, w_s, bias, w_out)


def _ffn_call(x, g, w_up, conv_w, w_down, final_g, *, final_norm):
    bsz, seq, d = x.shape
    d_ff = w_down.shape[0]
    rows = TILE_ROWS
    n_r = rows // SUBLANES
    n_slabs = 2 * d_ff // LANES
    x2 = x.reshape(bsz * seq, d)
    res = functools.partial(_resident, n_grid_axes=1)
    body = functools.partial(_ffn_kernel, final_norm=final_norm, col_block=256,
                             tiles_per_seq=seq // rows)
    out = pl.pallas_call(
        body,
        grid=(bsz * seq // rows,),
        in_specs=[pl.BlockSpec(memory_space=pl.ANY), res(g.shape), res(w_up.shape),
                  res(conv_w.shape), res(w_down.shape), res(final_g.shape)],
        out_specs=pl.BlockSpec(memory_space=pl.ANY),
        out_shape=jax.ShapeDtypeStruct(x2.shape, x.dtype),
        scratch_shapes=[
            pltpu.VMEM((2, n_r, SUBLANES, d), _F32),
            pltpu.VMEM((2, n_r, SUBLANES, d), _F32),
            pltpu.SemaphoreType.DMA((2, SUBLANES)),
            pltpu.SemaphoreType.DMA((2, SUBLANES)),
            pltpu.VMEM((rows, d), _BF16),
            pltpu.VMEM((n_slabs, HIST + rows, LANES), _F32),
            pltpu.VMEM((n_slabs, HIST, LANES), _F32),
            pltpu.VMEM((rows, d_ff), _BF16),
        ],
        compiler_params=pltpu.CompilerParams(
            dimension_semantics=("arbitrary",),
            vmem_limit_bytes=VMEM_LIMIT_BYTES),
        name="channel_mixer",
    )(x2, g, w_up, conv_w, w_down, final_g)
    return out.reshape(x.shape)


def kernel(x, mix_norm_g, w_in, conv_a_w, ln_v_g, ln_v_b, w_s, b_s, w_out, ffn_norm_g, w_up,
           conv_ffn_w, w_down, final_norm_g):
    depth, d = mix_norm_g.shape
    assert x.shape[1] % TILE_ROWS == 0 and TILE_ROWS % (2 * CHUNK) == 0
    gdim = d // SGU_GROUPS
    row = lambda a: a.reshape(1, -1)
    bias = jnp.repeat(jnp.swapaxes(b_s, 1, 2), gdim, axis=-1)
    for l in range(depth):
        x = _mixer_call(x, row(mix_norm_g[l]), w_in[l].astype(_BF16), conv_a_w[l],
                        row(ln_v_g[l]), row(ln_v_b[l]), w_s[l], bias[l], w_out[l].astype(_BF16))
        x = _ffn_call(x, row(ffn_norm_g[l]), w_up[l].astype(_BF16), conv_ffn_w[l],
                      w_down[l].astype(_BF16), row(final_norm_g), final_norm=(l == depth - 1))
    return x
```

```python
import functools

import jax
import jax.numpy as jnp
from jax import lax
from jax.experimental import pallas as pl
from jax.experimental.pallas import tpu as pltpu

EPS = 1e-6
CHUNK = 128
SGU_GROUPS = 8
CONV_WIDTH = 3
SUBLANES = 8
LANES = 128
TILE_ROWS = 512
VMEM_LIMIT_BYTES = 56 * 1024 * 1024

_BF16 = jnp.bfloat16
_F32 = jnp.float32


def _rmsnorm(x, g):
    ms = jnp.mean(x * x, axis=-1, keepdims=True)
    return x * lax.rsqrt(ms + EPS) * g


def _dot(a, b):
    return jnp.dot(a, b, preferred_element_type=_F32)


def _store_slabs(zbuf_ref, first_slab, row0, rows, value):
    for k in range(value.shape[1] // LANES):
        zbuf_ref[first_slab + k, row0:row0 + rows, :] = value[:, k * LANES:(k + 1) * LANES]


def _conv3_offset(zbuf_ref, w_ref, rows, slab):
    w = w_ref[:, slab * LANES:(slab + 1) * LANES]
    z2 = zbuf_ref[slab, SUBLANES - 2:SUBLANES - 2 + rows, :]
    z1 = zbuf_ref[slab, SUBLANES - 1:SUBLANES - 1 + rows, :]
    z0 = zbuf_ref[slab, SUBLANES:SUBLANES + rows, :]
    return z2 * w[0:1, :] + z1 * w[1:2, :] + z0 * w[2:3, :]


def _mixer_kernel(x_ref, g_ref, win_ref, cw_ref, lng_ref, lnb_ref, ws_ref, bias_ref, wout_ref,
                  o_ref, h_ref, zbuf_ref, vn_ref, mixed_ref, merged_ref):
    rows, d = x_ref.shape
    first_tile = pl.program_id(1) == 0
    col = lambda k: slice(k * d, (k + 1) * d)

    x = x_ref[...]
    h_ref[...] = _rmsnorm(x, g_ref[...]).astype(_BF16)
    proj = lambda k: _dot(h_ref[...], win_ref[:, col(k)])

    v = proj(4)
    mu = jnp.mean(v, axis=-1, keepdims=True)
    vc = v - mu
    var = jnp.mean(vc * vc, axis=-1, keepdims=True)
    vn_ref[...] = (vc * lax.rsqrt(var + EPS) * lng_ref[...] + lnb_ref[...]).astype(_BF16)

    prev = zbuf_ref[:, rows:rows + SUBLANES, :]
    zbuf_ref[:, 0:SUBLANES, :] = jnp.where(first_tile, jnp.zeros_like(prev), prev)
    _store_slabs(zbuf_ref, 0, SUBLANES, rows, proj(1) * proj(2))
    conv = jnp.concatenate([_conv3_offset(zbuf_ref, cw_ref, rows, s) for s in range(d // LANES)],
                           axis=1)
    y_a = proj(0) * conv
    merged_ref[...] = jax.nn.sigmoid(proj(5)) * y_a

    t_idx = lax.broadcasted_iota(jnp.int32, (CHUNK, CHUNK), 0)
    s_idx = lax.broadcasted_iota(jnp.int32, (CHUNK, CHUNK), 1)
    causal = s_idx <= t_idx
    gdim = d // SGU_GROUPS
    n_chunks = rows // CHUNK
    for g in range(SGU_GROUPS):
        wg = jnp.where(causal, ws_ref[g], 0.0).astype(_BF16)
        gcols = slice(g * gdim, (g + 1) * gdim)
        for n in range(0, n_chunks, 2):
            r0 = slice(n * CHUNK, (n + 1) * CHUNK)
            r1 = slice((n + 1) * CHUNK, (n + 2) * CHUNK)
            rhs = jnp.concatenate([vn_ref[r0, gcols], vn_ref[r1, gcols]], axis=1)
            res = _dot(wg, rhs)
            mixed_ref[r0, gcols] = res[:, :gdim]
            mixed_ref[r1, gcols] = res[:, gdim:]
    bias = bias_ref[...]
    for n in range(n_chunks):
        r = slice(n * CHUNK, (n + 1) * CHUNK)
        mixed_ref[r, :] = mixed_ref[r, :] + bias
    y_b = proj(3) * mixed_ref[...]
    merged_ref[...] = merged_ref[...] + jax.nn.sigmoid(proj(6)) * y_b

    o_ref[...] = x + _dot(merged_ref[...].astype(_BF16), wout_ref[...])


HIST = 2 * SUBLANES


def _tile_copies(hbm_ref, buf_ref, sem_ref, tile, slot, to_hbm):
    r = buf_ref.shape[1]
    copies = []
    for sub in range(SUBLANES):
        hbm = hbm_ref.at[pl.ds((tile * SUBLANES + sub) * r, r), :]
        vmem = buf_ref.at[slot, :, sub, :]
        src, dst = (vmem, hbm) if to_hbm else (hbm, vmem)
        copies.append(pltpu.make_async_copy(src, dst, sem_ref.at[slot, sub]))
    return copies


def _conv3_groups(zbuf_ref, w_ref, rows, slab):
    w = w_ref[:, slab * LANES:(slab + 1) * LANES]
    z = zbuf_ref[slab]
    z2, z1, z0 = z[0:rows], z[SUBLANES:SUBLANES + rows], z[HIST:HIST + rows]
    return z2 * w[0:1, :] + z1 * w[1:2, :] + z0 * w[2:3, :]


def _fill_history(zbuf_ref, pbuf_ref, rows, slab):
    last = zbuf_ref[slab, rows:rows + HIST, :]
    prev = pbuf_ref[slab]
    is_last_sublane = lax.broadcasted_iota(jnp.int32, (SUBLANES, LANES), 0) == SUBLANES - 1
    for k in range(2):
        rk = slice(k * SUBLANES, (k + 1) * SUBLANES)
        src = jnp.where(is_last_sublane, prev[rk], last[rk])
        zbuf_ref[slab, rk, :] = pltpu.roll(src, 1, 0)


def _ffn_kernel(x_hbm, g_ref, wup_ref, cw_ref, wdown_ref, fg_ref, o_hbm,
                xbuf, obuf, in_sem, out_sem, h_ref, zbuf_ref, pbuf_ref, act_ref,
                *, final_norm, col_block, tiles_per_seq):
    n_r, _, d = xbuf.shape[1:]
    rows = n_r * SUBLANES
    d_ff = wdown_ref.shape[0]
    slabs_per_block = col_block // LANES
    t = pl.program_id(0)
    n_tiles = pl.num_programs(0)
    slot = lax.rem(t, 2)
    starts_seq = lax.rem(t, tiles_per_seq) == 0

    @pl.when(t == 0)
    def _():
        zbuf_ref[:, rows:rows + HIST, :] = jnp.zeros((zbuf_ref.shape[0], HIST, LANES), _F32)
        for c in _tile_copies(x_hbm, xbuf, in_sem, 0, 0, to_hbm=False):
            c.start()

    @pl.when(t + 1 < n_tiles)
    def _():
        for c in _tile_copies(x_hbm, xbuf, in_sem, t + 1, 1 - slot, to_hbm=False):
            c.start()

    for c in _tile_copies(x_hbm, xbuf, in_sem, t, slot, to_hbm=False):
        c.wait()

    @pl.when(t >= 2)
    def _():
        for c in _tile_copies(o_hbm, obuf, out_sem, t - 2, slot, to_hbm=True):
            c.wait()

    x = xbuf[slot].reshape(rows, d)
    h_ref[...] = _rmsnorm(x, g_ref[...]).astype(_BF16)
    prev = zbuf_ref[:, rows:rows + HIST, :]
    pbuf_ref[...] = jnp.where(starts_seq, jnp.zeros_like(prev), prev)
    for j in range(d_ff // col_block):
        for c0 in (j * col_block, d_ff + j * col_block):
            up = _dot(h_ref[...], wup_ref[:, c0:c0 + col_block])
            _store_slabs(zbuf_ref, c0 // LANES, HIST, rows, up)
            for k in range(slabs_per_block):
                _fill_history(zbuf_ref, pbuf_ref, rows, c0 // LANES + k)
        for k in range(slabs_per_block):
            s = j * slabs_per_block + k
            gate = _conv3_groups(zbuf_ref, cw_ref, rows, s)
            val = _conv3_groups(zbuf_ref, cw_ref, rows, d_ff // LANES + s)
            act_ref[:, s * LANES:(s + 1) * LANES] = (gate * jax.nn.sigmoid(gate) * val).astype(_BF16)
    y = x + _dot(act_ref[...], wdown_ref[...])
    if final_norm:
        y = _rmsnorm(y, fg_ref[...])
    obuf[slot] = y.reshape(n_r, SUBLANES, d)

    for c in _tile_copies(o_hbm, obuf, out_sem, t, slot, to_hbm=True):
        c.start()

    @pl.when(t == n_tiles - 1)
    def _():
        for c in _tile_copies(o_hbm, obuf, out_sem, t, slot, to_hbm=True):
            c.wait()

    @pl.when(jnp.logical_and(t == n_tiles - 1, t >= 1))
    def _():
        for c in _tile_copies(o_hbm, obuf, out_sem, t - 1, 1 - slot, to_hbm=True):
            c.wait()


def _resident(shape, n_grid_axes):
    zeros = (0,) * len(shape)
    return pl.BlockSpec(shape, lambda *_: zeros, pipeline_mode=pl.Buffered(1))


def _mixer_call(x, g, w_in, conv_w, ln_g, ln_b, w_s, bias, w_out):
    bsz, seq, d = x.shape
    rows = TILE_ROWS
    tile = pl.BlockSpec((None, rows, d), lambda b, s: (b, s, 0))
    res = functools.partial(_resident, n_grid_axes=2)
    return pl.pallas_call(
        _mixer_kernel,
        grid=(bsz, seq // rows),
        in_specs=[tile, res(g.shape), res(w_in.shape), res(conv_w.shape), res(ln_g.shape),
                  res(ln_b.shape), res(w_s.shape), res(bias.shape), res(w_out.shape)],
        out_specs=tile,
        out_shape=jax.ShapeDtypeStruct(x.shape, x.dtype),
        scratch_shapes=[
            pltpu.VMEM((rows, d), _BF16),
            pltpu.VMEM((d // LANES, SUBLANES + rows, LANES), _F32),
            pltpu.VMEM((rows, d), _BF16),
            pltpu.VMEM((rows, d), _F32),
            pltpu.VMEM((rows, d), _F32),
        ],
        compiler_params=pltpu.CompilerParams(
            dimension_semantics=("arbitrary", "arbitrary"),
            vmem_limit_bytes=VMEM_LIMIT_BYTES),
        name="token_mixer",
    )(x, g, w_in, conv_w, ln_g, ln_b, w_s, bias, w_out)


def _ffn_call(x, g, w_up, conv_w, w_down, final_g, *, final_norm):
    bsz, seq, d = x.shape
    d_ff = w_down.shape[0]
    rows = TILE_ROWS
    n_r = rows // SUBLANES
    n_slabs = 2 * d_ff // LANES
    x2 = x.reshape(bsz * seq, d)
    res = functools.partial(_resident, n_grid_axes=1)
    body = functools.partial(_ffn_kernel, final_norm=final_norm, col_block=256,
                             tiles_per_seq=seq // rows)
    out = pl.pallas_call(
        body,
        grid=(bsz * seq // rows,),
        in_specs=[pl.BlockSpec(memory_space=pl.ANY), res(g.shape), res(w_up.shape),
                  res(conv_w.shape), res(w_down.shape), res(final_g.shape)],
        out_specs=pl.BlockSpec(memory_space=pl.ANY),
        out_shape=jax.ShapeDtypeStruct(x2.shape, x.dtype),
        scratch_shapes=[
            pltpu.VMEM((2, n_r, SUBLANES, d), _F32),
            pltpu.VMEM((2, n_r, SUBLANES, d), _F32),
            pltpu.SemaphoreType.DMA((2, SUBLANES)),
            pltpu.SemaphoreType.DMA((2, SUBLANES)),
            pltpu.VMEM((rows, d), _BF16),
            pltpu.VMEM((n_slabs, HIST + rows, LANES), _F32),
            pltpu.VMEM((n_slabs, HIST, LANES), _F32),
            pltpu.VMEM((rows, d_ff), _BF16),
        ],
        compiler_params=pltpu.CompilerParams(
            dimension_semantics=("arbitrary",),
            vmem_limit_bytes=VMEM_LIMIT_BYTES),
        name="channel_mixer",
    )(x2, g, w_up, conv_w, w_down, final_g)
    return out.reshape(x.shape)


def kernel(x, mix_norm_g, w_in, conv_a_w, ln_v_g, ln_v_b, w_s, b_s, w_out, ffn_norm_g, w_up,
           conv_ffn_w, w_down, final_norm_g):
    depth, d = mix_norm_g.shape
    assert x.shape[1] % TILE_ROWS == 0 and TILE_ROWS % (2 * CHUNK) == 0
    gdim = d // SGU_GROUPS
    row = lambda a: a.reshape(1, -1)
    bias = jnp.repeat(jnp.swapaxes(b_s, 1, 2), gdim, axis=-1)
    for l in range(depth):
        x = _mixer_call(x, row(mix_norm_g[l]), w_in[l].astype(_BF16), conv_a_w[l],
                        row(ln_v_g[l]), row(ln_v_b[l]), w_s[l], bias[l], w_out[l].astype(_BF16))
        x = _ffn_call(x, row(ffn_norm_g[l]), w_up[l].astype(_BF16), conv_ffn_w[l],
                      w_down[l].astype(_BF16), row(final_norm_g), final_norm=(l == depth - 1))
    return x
```

```python
import functools

import jax
import jax.numpy as jnp
from jax import lax
from jax.experimental import pallas as pl
from jax.experimental.pallas import tpu as pltpu

EPS = 1e-6
CHUNK = 128
SGU_GROUPS = 8
CONV_WIDTH = 3
SUBLANES = 8
LANES = 128
TILE_ROWS = 512
VMEM_LIMIT_BYTES = 56 * 1024 * 1024

_BF16 = jnp.bfloat16
_F32 = jnp.float32


def _rmsnorm(x, g):
    ms = jnp.mean(x * x, axis=-1, keepdims=True)
    return x * lax.rsqrt(ms + EPS) * g


def _dot(a, b):
    return jnp.dot(a, b, preferred_element_type=_F32)


def _store_slabs(zbuf_ref, first_slab, row0, rows, value):
    for k in range(value.shape[1] // LANES):
        zbuf_ref[first_slab + k, row0:row0 + rows, :] = value[:, k * LANES:(k + 1) * LANES]


def _conv3_offset(zbuf_ref, w_ref, rows, slab):
    w = w_ref[:, slab * LANES:(slab + 1) * LANES]
    z2 = zbuf_ref[slab, SUBLANES - 2:SUBLANES - 2 + rows, :]
    z1 = zbuf_ref[slab, SUBLANES - 1:SUBLANES - 1 + rows, :]
    z0 = zbuf_ref[slab, SUBLANES:SUBLANES + rows, :]
    return z2 * w[0:1, :] + z1 * w[1:2, :] + z0 * w[2:3, :]


def _mixer_kernel(x_ref, g_ref, win_ref, cw_ref, lng_ref, lnb_ref, ws_ref, bias_ref, wout_ref,
                  o_ref, h_ref, zbuf_ref, vn_ref, mixed_ref, merged_ref):
    rows, d = x_ref.shape
    first_tile = pl.program_id(1) == 0
    col = lambda k: slice(k * d, (k + 1) * d)

    x = x_ref[...]
    h_ref[...] = _rmsnorm(x, g_ref[...]).astype(_BF16)
    proj = lambda k: _dot(h_ref[...], win_ref[:, col(k)])

    v = proj(4)
    mu = jnp.mean(v, axis=-1, keepdims=True)
    vc = v - mu
    var = jnp.mean(vc * vc, axis=-1, keepdims=True)
    vn_ref[...] = (vc * lax.rsqrt(var + EPS) * lng_ref[...] + lnb_ref[...]).astype(_BF16)

    prev = zbuf_ref[:, rows:rows + SUBLANES, :]
    zbuf_ref[:, 0:SUBLANES, :] = jnp.where(first_tile, jnp.zeros_like(prev), prev)
    _store_slabs(zbuf_ref, 0, SUBLANES, rows, proj(1) * proj(2))
    conv = jnp.concatenate([_conv3_offset(zbuf_ref, cw_ref, rows, s) for s in range(d // LANES)],
                           axis=1)
    y_a = proj(0) * conv
    merged_ref[...] = jax.nn.sigmoid(proj(5)) * y_a

    t_idx = lax.broadcasted_iota(jnp.int32, (CHUNK, CHUNK), 0)
    s_idx = lax.broadcasted_iota(jnp.int32, (CHUNK, CHUNK), 1)
    causal = s_idx <= t_idx
    gdim = d // SGU_GROUPS
    n_chunks = rows // CHUNK
    for g in range(SGU_GROUPS):
        wg = jnp.where(causal, ws_ref[g], 0.0).astype(_BF16)
        gcols = slice(g * gdim, (g + 1) * gdim)
        for n in range(0, n_chunks, 2):
            r0 = slice(n * CHUNK, (n + 1) * CHUNK)
            r1 = slice((n + 1) * CHUNK, (n + 2) * CHUNK)
            rhs = jnp.concatenate([vn_ref[r0, gcols], vn_ref[r1, gcols]], axis=1)
            res = _dot(wg, rhs)
            mixed_ref[r0, gcols] = res[:, :gdim]
            mixed_ref[r1, gcols] = res[:, gdim:]
    bias = bias_ref[...]
    for n in range(n_chunks):
        r = slice(n * CHUNK, (n + 1) * CHUNK)
        mixed_ref[r, :] = mixed_ref[r, :] + bias
    y_b = proj(3) * mixed_ref[...]
    merged_ref[...] = merged_ref[...] + jax.nn.sigmoid(proj(6)) * y_b

    o_ref[...] = x + _dot(merged_ref[...].astype(_BF16), wout_ref[...])


HIST = 2 * SUBLANES


def _tile_copies(hbm_ref, buf_ref, sem_ref, tile, slot, to_hbm):
    r = buf_ref.shape[1]
    copies = []
    for sub in range(SUBLANES):
        hbm = hbm_ref.at[pl.ds((tile * SUBLANES + sub) * r, r), :]
        vmem = buf_ref.at[slot, :, sub, :]
        src, dst = (vmem, hbm) if to_hbm else (hbm, vmem)
        copies.append(pltpu.make_async_copy(src, dst, sem_ref.at[slot, sub]))
    return copies


def _conv3_groups(zbuf_ref, w_ref, rows, slab):
    w = w_ref[:, slab * LANES:(slab + 1) * LANES]
    z = zbuf_ref[slab]
    z2, z1, z0 = z[0:rows], z[SUBLANES:SUBLANES + rows], z[HIST:HIST + rows]
    return z2 * w[0:1, :] + z1 * w[1:2, :] + z0 * w[2:3, :]


def _fill_history(zbuf_ref, pbuf_ref, rows, slab):
    last = zbuf_ref[slab, rows:rows + HIST, :]
    prev = pbuf_ref[slab]
    is_last_sublane = lax.broadcasted_iota(jnp.int32, (SUBLANES, LANES), 0) == SUBLANES - 1
    for k in range(2):
        rk = slice(k * SUBLANES, (k + 1) * SUBLANES)
        src = jnp.where(is_last_sublane, prev[rk], last[rk])
        zbuf_ref[slab, rk, :] = pltpu.roll(src, 1, 0)


def _ffn_kernel(x_hbm, g_ref, wup_ref, cw_ref, wdown_ref, fg_ref, o_hbm,
                xbuf, obuf, in_sem, out_sem, h_ref, zbuf_ref, pbuf_ref, act_ref,
                *, final_norm, col_block, tiles_per_seq):
    n_r, _, d = xbuf.shape[1:]
    rows = n_r * SUBLANES
    d_ff = wdown_ref.shape[0]
    slabs_per_block = col_block // LANES
    t = pl.program_id(0)
    n_tiles = pl.num_programs(0)
    slot = lax.rem(t, 2)
    starts_seq = lax.rem(t, tiles_per_seq) == 0

    @pl.when(t == 0)
    def _():
        zbuf_ref[:, rows:rows + HIST, :] = jnp.zeros((zbuf_ref.shape[0], HIST, LANES), _F32)
        for c in _tile_copies(x_hbm, xbuf, in_sem, 0, 0, to_hbm=False):
            c.start()

    @pl.when(t + 1 < n_tiles)
    def _():
        for c in _tile_copies(x_hbm, xbuf, in_sem, t + 1, 1 - slot, to_hbm=False):
            c.start()

    for c in _tile_copies(x_hbm, xbuf, in_sem, t, slot, to_hbm=False):
        c.wait()

    @pl.when(t >= 2)
    def _():
        for c in _tile_copies(o_hbm, obuf, out_sem, t - 2, slot, to_hbm=True):
            c.wait()

    x = xbuf[slot].reshape(rows, d)
    h_ref[...] = _rmsnorm(x, g_ref[...]).astype(_BF16)
    prev = zbuf_ref[:, rows:rows + HIST, :]
    pbuf_ref[...] = jnp.where(starts_seq, jnp.zeros_like(prev), prev)
    for j in range(d_ff // col_block):
        for c0 in (j * col_block, d_ff + j * col_block):
            up = _dot(h_ref[...], wup_ref[:, c0:c0 + col_block])
            _store_slabs(zbuf_ref, c0 // LANES, HIST, rows, up)
            for k in range(slabs_per_block):
                _fill_history(zbuf_ref, pbuf_ref, rows, c0 // LANES + k)
        for k in range(slabs_per_block):
            s = j * slabs_per_block + k
            gate = _conv3_groups(zbuf_ref, cw_ref, rows, s)
            val = _conv3_groups(zbuf_ref, cw_ref, rows, d_ff // LANES + s)
            act_ref[:, s * LANES:(s + 1) * LANES] = (gate * jax.nn.sigmoid(gate) * val).astype(_BF16)
    y = x + _dot(act_ref[...], wdown_ref[...])
    if final_norm:
        y = _rmsnorm(y, fg_ref[...])
    obuf[slot] = y.reshape(n_r, SUBLANES, d)

    for c in _tile_copies(o_hbm, obuf, out_sem, t, slot, to_hbm=True):
        c.start()

    @pl.when(t == n_tiles - 1)
    def _():
        for c in _tile_copies(o_hbm, obuf, out_sem, t, slot, to_hbm=True):
            c.wait()

    @pl.when(jnp.logical_and(t == n_tiles - 1, t >= 1))
    def _():
        for c in _tile_copies(o_hbm, obuf, out_sem, t - 1, 1 - slot, to_hbm=True):
            c.wait()


def _layer(stacked, layer):
    index = (layer,) + (0,) * (stacked.ndim - 1)
    return pl.BlockSpec((None,) + stacked.shape[1:], lambda *_: index, pipeline_mode=pl.Buffered(1))


def _mixer_call(x, layer, g, w_in, conv_w, ln_g, ln_b, w_s, bias, w_out):
    bsz, seq, d = x.shape
    rows = TILE_ROWS
    tile = pl.BlockSpec((None, rows, d), lambda b, s: (b, s, 0))
    params = (g, w_in, conv_w, ln_g, ln_b, w_s, bias, w_out)
    return pl.pallas_call(
        _mixer_kernel,
        grid=(bsz, seq // rows),
        in_specs=[tile] + [_layer(p, layer) for p in params],
        out_specs=tile,
        out_shape=jax.ShapeDtypeStruct(x.shape, x.dtype),
        scratch_shapes=[
            pltpu.VMEM((rows, d), _BF16),
            pltpu.VMEM((d // LANES, SUBLANES + rows, LANES), _F32),
            pltpu.VMEM((rows, d), _BF16),
            pltpu.VMEM((rows, d), _F32),
            pltpu.VMEM((rows, d), _F32),
        ],
        compiler_params=pltpu.CompilerParams(
            dimension_semantics=("arbitrary", "arbitrary"),
            vmem_limit_bytes=VMEM_LIMIT_BYTES),
        name="token_mixer",
    )(x, *params)


def _ffn_call(x, layer, g, w_up, conv_w, w_down, final_g, *, final_norm):
    bsz, seq, d = x.shape
    d_ff = w_down.shape[1]
    rows = TILE_ROWS
    n_r = rows // SUBLANES
    n_slabs = 2 * d_ff // LANES
    x2 = x.reshape(bsz * seq, d)
    params = (g, w_up, conv_w, w_down)
    body = functools.partial(_ffn_kernel, final_norm=final_norm, col_block=256,
                             tiles_per_seq=seq // rows)
    out = pl.pallas_call(
        body,
        grid=(bsz * seq // rows,),
        in_specs=[pl.BlockSpec(memory_space=pl.ANY)] + [_layer(p, layer) for p in params]
                 + [_layer(final_g, 0)],
        out_specs=pl.BlockSpec(memory_space=pl.ANY),
        out_shape=jax.ShapeDtypeStruct(x2.shape, x.dtype),
        scratch_shapes=[
            pltpu.VMEM((2, n_r, SUBLANES, d), _F32),
            pltpu.VMEM((2, n_r, SUBLANES, d), _F32),
            pltpu.SemaphoreType.DMA((2, SUBLANES)),
            pltpu.SemaphoreType.DMA((2, SUBLANES)),
            pltpu.VMEM((rows, d), _BF16),
            pltpu.VMEM((n_slabs, HIST + rows, LANES), _F32),
            pltpu.VMEM((n_slabs, HIST, LANES), _F32),
            pltpu.VMEM((rows, d_ff), _BF16),
        ],
        compiler_params=pltpu.CompilerParams(
            dimension_semantics=("arbitrary",),
            vmem_limit_bytes=VMEM_LIMIT_BYTES),
        name="channel_mixer",
    )(x2, *params, final_g)
    return out.reshape(x.shape)


def kernel(x, mix_norm_g, w_in, conv_a_w, ln_v_g, ln_v_b, w_s, b_s, w_out, ffn_norm_g, w_up,
           conv_ffn_w, w_down, final_norm_g):
    depth, d = mix_norm_g.shape
    assert x.shape[1] % TILE_ROWS == 0 and TILE_ROWS % (2 * CHUNK) == 0
    gdim = d // SGU_GROUPS
    rows_of = lambda a: a.reshape(a.shape[0], 1, a.shape[1])
    bias = jnp.repeat(jnp.swapaxes(b_s, 1, 2), gdim, axis=-1)
    mixer_params = (rows_of(mix_norm_g), w_in.astype(_BF16), conv_a_w, rows_of(ln_v_g),
                    rows_of(ln_v_b), w_s, bias, w_out.astype(_BF16))
    ffn_params = (rows_of(ffn_norm_g), w_up.astype(_BF16), conv_ffn_w, w_down.astype(_BF16),
                  final_norm_g.reshape(1, 1, d))
    for l in range(depth):
        x = _mixer_call(x, l, *mixer_params)
        x = _ffn_call(x, l, *ffn_params, final_norm=(l == depth - 1))
    return x
```

```python
import functools

import jax
import jax.numpy as jnp
from jax import lax
from jax.experimental import pallas as pl
from jax.experimental.pallas import tpu as pltpu

EPS = 1e-6
CHUNK = 128
SGU_GROUPS = 8
CONV_WIDTH = 3
SUBLANES = 8
LANES = 128
TILE_ROWS = 512
MIXER_TILE_ROWS = 1024
VMEM_LIMIT_BYTES = 56 * 1024 * 1024

_BF16 = jnp.bfloat16
_F32 = jnp.float32


def _rmsnorm(x, g):
    ms = jnp.mean(x * x, axis=-1, keepdims=True)
    return x * lax.rsqrt(ms + EPS) * g


def _dot(a, b):
    return jnp.dot(a, b, preferred_element_type=_F32)


def _store_slabs(zbuf_ref, first_slab, row0, rows, value):
    for k in range(value.shape[1] // LANES):
        zbuf_ref[first_slab + k, row0:row0 + rows, :] = value[:, k * LANES:(k + 1) * LANES]


def _conv3_offset(zbuf_ref, w_ref, rows, slab):
    w = w_ref[:, slab * LANES:(slab + 1) * LANES]
    z2 = zbuf_ref[slab, SUBLANES - 2:SUBLANES - 2 + rows, :]
    z1 = zbuf_ref[slab, SUBLANES - 1:SUBLANES - 1 + rows, :]
    z0 = zbuf_ref[slab, SUBLANES:SUBLANES + rows, :]
    return z2 * w[0:1, :] + z1 * w[1:2, :] + z0 * w[2:3, :]


def _mixer_kernel(x_ref, g_ref, win_ref, cw_ref, lng_ref, lnb_ref, ws_ref, bias_ref, wout_ref,
                  o_ref, h_ref, zbuf_ref, vn_ref, mixed_ref, merged_ref):
    rows, d = x_ref.shape
    first_tile = pl.program_id(1) == 0
    col = lambda k: slice(k * d, (k + 1) * d)

    x = x_ref[...]
    h_ref[...] = _rmsnorm(x, g_ref[...]).astype(_BF16)
    proj = lambda k: _dot(h_ref[...], win_ref[:, col(k)])

    v = proj(4)
    mu = jnp.mean(v, axis=-1, keepdims=True)
    vc = v - mu
    var = jnp.mean(vc * vc, axis=-1, keepdims=True)
    vn_ref[...] = (vc * lax.rsqrt(var + EPS) * lng_ref[...] + lnb_ref[...]).astype(_BF16)

    prev = zbuf_ref[:, rows:rows + SUBLANES, :]
    zbuf_ref[:, 0:SUBLANES, :] = jnp.where(first_tile, jnp.zeros_like(prev), prev)
    _store_slabs(zbuf_ref, 0, SUBLANES, rows, proj(1) * proj(2))
    conv = jnp.concatenate([_conv3_offset(zbuf_ref, cw_ref, rows, s) for s in range(d // LANES)],
                           axis=1)
    y_a = proj(0) * conv
    merged_ref[...] = jax.nn.sigmoid(proj(5)) * y_a

    t_idx = lax.broadcasted_iota(jnp.int32, (CHUNK, CHUNK), 0)
    s_idx = lax.broadcasted_iota(jnp.int32, (CHUNK, CHUNK), 1)
    causal = s_idx <= t_idx
    gdim = d // SGU_GROUPS
    n_chunks = rows // CHUNK
    for g in range(SGU_GROUPS):
        wg = jnp.where(causal, ws_ref[g], 0.0).astype(_BF16)
        gcols = slice(g * gdim, (g + 1) * gdim)
        for n in range(0, n_chunks, 2):
            r0 = slice(n * CHUNK, (n + 1) * CHUNK)
            r1 = slice((n + 1) * CHUNK, (n + 2) * CHUNK)
            rhs = jnp.concatenate([vn_ref[r0, gcols], vn_ref[r1, gcols]], axis=1)
            res = _dot(wg, rhs)
            mixed_ref[r0, gcols] = res[:, :gdim]
            mixed_ref[r1, gcols] = res[:, gdim:]
    bias = bias_ref[...]
    for n in range(n_chunks):
        r = slice(n * CHUNK, (n + 1) * CHUNK)
        mixed_ref[r, :] = mixed_ref[r, :] + bias
    y_b = proj(3) * mixed_ref[...]
    merged_ref[...] = merged_ref[...] + jax.nn.sigmoid(proj(6)) * y_b

    o_ref[...] = x + _dot(merged_ref[...].astype(_BF16), wout_ref[...])


HIST = 2 * SUBLANES


def _tile_copies(hbm_ref, buf_ref, sem_ref, tile, slot, to_hbm):
    r = buf_ref.shape[1]
    copies = []
    for sub in range(SUBLANES):
        hbm = hbm_ref.at[pl.ds((tile * SUBLANES + sub) * r, r), :]
        vmem = buf_ref.at[slot, :, sub, :]
        src, dst = (vmem, hbm) if to_hbm else (hbm, vmem)
        copies.append(pltpu.make_async_copy(src, dst, sem_ref.at[slot, sub]))
    return copies


def _conv3_groups(zbuf_ref, w_ref, rows, slab):
    w = w_ref[:, slab * LANES:(slab + 1) * LANES]
    z = zbuf_ref[slab]
    z2, z1, z0 = z[0:rows], z[SUBLANES:SUBLANES + rows], z[HIST:HIST + rows]
    return z2 * w[0:1, :] + z1 * w[1:2, :] + z0 * w[2:3, :]


def _fill_history(zbuf_ref, pbuf_ref, rows, slab):
    last = zbuf_ref[slab, rows:rows + HIST, :]
    prev = pbuf_ref[slab]
    is_last_sublane = lax.broadcasted_iota(jnp.int32, (SUBLANES, LANES), 0) == SUBLANES - 1
    for k in range(2):
        rk = slice(k * SUBLANES, (k + 1) * SUBLANES)
        src = jnp.where(is_last_sublane, prev[rk], last[rk])
        zbuf_ref[slab, rk, :] = pltpu.roll(src, 1, 0)


def _ffn_kernel(x_hbm, g_ref, wup_ref, cw_ref, wdown_ref, fg_ref, o_hbm,
                xbuf, obuf, in_sem, out_sem, h_ref, zbuf_ref, pbuf_ref, act_ref,
                *, final_norm, col_block, tiles_per_seq):
    n_r, _, d = xbuf.shape[1:]
    rows = n_r * SUBLANES
    d_ff = wdown_ref.shape[0]
    slabs_per_block = col_block // LANES
    t = pl.program_id(0)
    n_tiles = pl.num_programs(0)
    slot = lax.rem(t, 2)
    starts_seq = lax.rem(t, tiles_per_seq) == 0

    @pl.when(t == 0)
    def _():
        zbuf_ref[:, rows:rows + HIST, :] = jnp.zeros((zbuf_ref.shape[0], HIST, LANES), _F32)
        for c in _tile_copies(x_hbm, xbuf, in_sem, 0, 0, to_hbm=False):
            c.start()

    @pl.when(t + 1 < n_tiles)
    def _():
        for c in _tile_copies(x_hbm, xbuf, in_sem, t + 1, 1 - slot, to_hbm=False):
            c.start()

    for c in _tile_copies(x_hbm, xbuf, in_sem, t, slot, to_hbm=False):
        c.wait()

    @pl.when(t >= 2)
    def _():
        for c in _tile_copies(o_hbm, obuf, out_sem, t - 2, slot, to_hbm=True):
            c.wait()

    x = xbuf[slot].reshape(rows, d)
    h_ref[...] = _rmsnorm(x, g_ref[...]).astype(_BF16)
    prev = zbuf_ref[:, rows:rows + HIST, :]
    pbuf_ref[...] = jnp.where(starts_seq, jnp.zeros_like(prev), prev)
    for j in range(d_ff // col_block):
        for c0 in (j * col_block, d_ff + j * col_block):
            up = _dot(h_ref[...], wup_ref[:, c0:c0 + col_block])
            _store_slabs(zbuf_ref, c0 // LANES, HIST, rows, up)
            for k in range(slabs_per_block):
                _fill_history(zbuf_ref, pbuf_ref, rows, c0 // LANES + k)
        for k in range(slabs_per_block):
            s = j * slabs_per_block + k
            gate = _conv3_groups(zbuf_ref, cw_ref, rows, s)
            val = _conv3_groups(zbuf_ref, cw_ref, rows, d_ff // LANES + s)
            act_ref[:, s * LANES:(s + 1) * LANES] = (gate * jax.nn.sigmoid(gate) * val).astype(_BF16)
    y = x + _dot(act_ref[...], wdown_ref[...])
    if final_norm:
        y = _rmsnorm(y, fg_ref[...])
    obuf[slot] = y.reshape(n_r, SUBLANES, d)

    for c in _tile_copies(o_hbm, obuf, out_sem, t, slot, to_hbm=True):
        c.start()

    @pl.when(t == n_tiles - 1)
    def _():
        for c in _tile_copies(o_hbm, obuf, out_sem, t, slot, to_hbm=True):
            c.wait()

    @pl.when(jnp.logical_and(t == n_tiles - 1, t >= 1))
    def _():
        for c in _tile_copies(o_hbm, obuf, out_sem, t - 1, 1 - slot, to_hbm=True):
            c.wait()


def _layer(stacked, layer):
    index = (layer,) + (0,) * (stacked.ndim - 1)
    return pl.BlockSpec((None,) + stacked.shape[1:], lambda *_: index, pipeline_mode=pl.Buffered(1))


def _mixer_call(x, layer, g, w_in, conv_w, ln_g, ln_b, w_s, bias, w_out):
    bsz, seq, d = x.shape
    rows = MIXER_TILE_ROWS
    tile = pl.BlockSpec((None, rows, d), lambda b, s: (b, s, 0))
    params = (g, w_in, conv_w, ln_g, ln_b, w_s, bias, w_out)
    return pl.pallas_call(
        _mixer_kernel,
        grid=(bsz, seq // rows),
        in_specs=[tile] + [_layer(p, layer) for p in params],
        out_specs=tile,
        out_shape=jax.ShapeDtypeStruct(x.shape, x.dtype),
        scratch_shapes=[
            pltpu.VMEM((rows, d), _BF16),
            pltpu.VMEM((d // LANES, SUBLANES + rows, LANES), _F32),
            pltpu.VMEM((rows, d), _BF16),
            pltpu.VMEM((rows, d), _F32),
            pltpu.VMEM((rows, d), _F32),
        ],
        compiler_params=pltpu.CompilerParams(
            dimension_semantics=("arbitrary", "arbitrary"),
            vmem_limit_bytes=VMEM_LIMIT_BYTES),
        name="token_mixer",
    )(x, *params)


def _ffn_call(x, layer, g, w_up, conv_w, w_down, final_g, *, final_norm):
    bsz, seq, d = x.shape
    d_ff = w_down.shape[1]
    rows = TILE_ROWS
    n_r = rows // SUBLANES
    n_slabs = 2 * d_ff // LANES
    x2 = x.reshape(bsz * seq, d)
    params = (g, w_up, conv_w, w_down)
    body = functools.partial(_ffn_kernel, final_norm=final_norm, col_block=256,
                             tiles_per_seq=seq // rows)
    out = pl.pallas_call(
        body,
        grid=(bsz * seq // rows,),
        in_specs=[pl.BlockSpec(memory_space=pl.ANY)] + [_layer(p, layer) for p in params]
                 + [_layer(final_g, 0)],
        out_specs=pl.BlockSpec(memory_space=pl.ANY),
        out_shape=jax.ShapeDtypeStruct(x2.shape, x.dtype),
        scratch_shapes=[
            pltpu.VMEM((2, n_r, SUBLANES, d), _F32),
            pltpu.VMEM((2, n_r, SUBLANES, d), _F32),
            pltpu.SemaphoreType.DMA((2, SUBLANES)),
            pltpu.SemaphoreType.DMA((2, SUBLANES)),
            pltpu.VMEM((rows, d), _BF16),
            pltpu.VMEM((n_slabs, HIST + rows, LANES), _F32),
            pltpu.VMEM((n_slabs, HIST, LANES), _F32),
            pltpu.VMEM((rows, d_ff), _BF16),
        ],
        compiler_params=pltpu.CompilerParams(
            dimension_semantics=("arbitrary",),
            vmem_limit_bytes=VMEM_LIMIT_BYTES),
        name="channel_mixer",
    )(x2, *params, final_g)
    return out.reshape(x.shape)


def kernel(x, mix_norm_g, w_in, conv_a_w, ln_v_g, ln_v_b, w_s, b_s, w_out, ffn_norm_g, w_up,
           conv_ffn_w, w_down, final_norm_g):
    depth, d = mix_norm_g.shape
    assert x.shape[1] % TILE_ROWS == 0 and x.shape[1] % MIXER_TILE_ROWS == 0
    assert MIXER_TILE_ROWS % (2 * CHUNK) == 0
    gdim = d // SGU_GROUPS
    rows_of = lambda a: a.reshape(a.shape[0], 1, a.shape[1])
    bias = jnp.repeat(jnp.swapaxes(b_s, 1, 2), gdim, axis=-1)
    mixer_params = (rows_of(mix_norm_g), w_in.astype(_BF16), conv_a_w, rows_of(ln_v_g),
                    rows_of(ln_v_b), w_s, bias, w_out.astype(_BF16))
    ffn_params = (rows_of(ffn_norm_g), w_up.astype(_BF16), conv_ffn_w, w_down.astype(_BF16),
                  final_norm_g.reshape(1, 1, d))
    for l in range(depth):
        x = _mixer_call(x, l, *mixer_params)
        x = _ffn_call(x, l, *ffn_params, final_norm=(l == depth - 1))
    return x
```

```python
import functools

import jax
import jax.numpy as jnp
from jax import lax
from jax.experimental import pallas as pl
from jax.experimental.pallas import tpu as pltpu

EPS = 1e-6
CHUNK = 128
SGU_GROUPS = 8
CONV_WIDTH = 3
SUBLANES = 8
LANES = 128
BF16_SUBLANES = 16
TILE_ROWS = 512
MIXER_TILE_ROWS = 1024
VMEM_LIMIT_BYTES = 56 * 1024 * 1024

_BF16 = jnp.bfloat16
_F32 = jnp.float32


def _rmsnorm(x, g):
    ms = jnp.mean(x * x, axis=-1, keepdims=True)
    return x * lax.rsqrt(ms + EPS) * g


def _dot(a, b):
    return jnp.dot(a, b, preferred_element_type=_F32)


def _store_slabs(zbuf_ref, first_slab, row0, rows, value):
    for k in range(value.shape[1] // LANES):
        zbuf_ref[first_slab + k, row0:row0 + rows, :] = value[:, k * LANES:(k + 1) * LANES]


def _conv3_offset(zbuf_ref, w_ref, rows, slab):
    w = w_ref[:, slab * LANES:(slab + 1) * LANES]
    z2 = zbuf_ref[slab, SUBLANES - 2:SUBLANES - 2 + rows, :]
    z1 = zbuf_ref[slab, SUBLANES - 1:SUBLANES - 1 + rows, :]
    z0 = zbuf_ref[slab, SUBLANES:SUBLANES + rows, :]
    return z2 * w[0:1, :] + z1 * w[1:2, :] + z0 * w[2:3, :]


def _mixer_kernel(x_ref, g_ref, win_ref, cw_ref, lng_ref, lnb_ref, ws_ref, bias_ref, wout_ref,
                  o_ref, h_ref, zbuf_ref, vn_ref, mixed_ref, merged_ref):
    rows, d = x_ref.shape
    first_tile = pl.program_id(1) == 0
    col = lambda k: slice(k * d, (k + 1) * d)

    x = x_ref[...]
    h_ref[...] = _rmsnorm(x, g_ref[...]).astype(_BF16)
    proj = lambda k: _dot(h_ref[...], win_ref[:, col(k)])

    v = proj(4)
    mu = jnp.mean(v, axis=-1, keepdims=True)
    vc = v - mu
    var = jnp.mean(vc * vc, axis=-1, keepdims=True)
    vn_ref[...] = (vc * lax.rsqrt(var + EPS) * lng_ref[...] + lnb_ref[...]).astype(_BF16)

    prev = zbuf_ref[:, rows:rows + SUBLANES, :]
    zbuf_ref[:, 0:SUBLANES, :] = jnp.where(first_tile, jnp.zeros_like(prev), prev)
    _store_slabs(zbuf_ref, 0, SUBLANES, rows, proj(1) * proj(2))
    conv = jnp.concatenate([_conv3_offset(zbuf_ref, cw_ref, rows, s) for s in range(d // LANES)],
                           axis=1)
    y_a = proj(0) * conv
    merged_ref[...] = jax.nn.sigmoid(proj(5)) * y_a

    t_idx = lax.broadcasted_iota(jnp.int32, (CHUNK, CHUNK), 0)
    s_idx = lax.broadcasted_iota(jnp.int32, (CHUNK, CHUNK), 1)
    causal = s_idx <= t_idx
    gdim = d // SGU_GROUPS
    n_chunks = rows // CHUNK
    for g in range(SGU_GROUPS):
        wg = jnp.where(causal, ws_ref[g], 0.0).astype(_BF16)
        gcols = slice(g * gdim, (g + 1) * gdim)
        for n in range(0, n_chunks, 2):
            r0 = slice(n * CHUNK, (n + 1) * CHUNK)
            r1 = slice((n + 1) * CHUNK, (n + 2) * CHUNK)
            rhs = jnp.concatenate([vn_ref[r0, gcols], vn_ref[r1, gcols]], axis=1)
            res = _dot(wg, rhs)
            mixed_ref[r0, gcols] = res[:, :gdim]
            mixed_ref[r1, gcols] = res[:, gdim:]
    mixed = mixed_ref[...].reshape(n_chunks, CHUNK, d) + bias_ref[...][None]
    y_b = proj(3) * mixed.reshape(rows, d)
    merged = merged_ref[...] + jax.nn.sigmoid(proj(6)) * y_b

    o_ref[...] = x + _dot(merged.astype(_BF16), wout_ref[...])


HIST = 2 * SUBLANES


def _tile_copies(hbm_ref, buf_ref, sem_ref, tile, slot, to_hbm):
    r = buf_ref.shape[1]
    copies = []
    for sub in range(SUBLANES):
        hbm = hbm_ref.at[pl.ds((tile * SUBLANES + sub) * r, r), :]
        vmem = buf_ref.at[slot, :, sub, :]
        src, dst = (vmem, hbm) if to_hbm else (hbm, vmem)
        copies.append(pltpu.make_async_copy(src, dst, sem_ref.at[slot, sub]))
    return copies


def _conv3_groups(zbuf_ref, w_ref, rows, slab):
    w = w_ref[:, slab * LANES:(slab + 1) * LANES]
    z = zbuf_ref[slab]
    z2, z1, z0 = z[0:rows], z[SUBLANES:SUBLANES + rows], z[HIST:HIST + rows]
    return z2 * w[0:1, :] + z1 * w[1:2, :] + z0 * w[2:3, :]


def _fill_history(zbuf_ref, pbuf_ref, rows, slab):
    last = zbuf_ref[slab, rows:rows + HIST, :]
    prev = pbuf_ref[slab]
    is_last_sublane = lax.broadcasted_iota(jnp.int32, (SUBLANES, LANES), 0) == SUBLANES - 1
    for k in range(2):
        rk = slice(k * SUBLANES, (k + 1) * SUBLANES)
        src = jnp.where(is_last_sublane, prev[rk], last[rk])
        zbuf_ref[slab, rk, :] = pltpu.roll(src, 1, 0)


def _ffn_kernel(x_hbm, g_ref, wup_ref, cw_ref, wdown_ref, fg_ref, o_hbm,
                xbuf, obuf, in_sem, out_sem, h_ref, zbuf_ref, pbuf_ref, act_ref,
                *, final_norm, col_block, tiles_per_seq):
    n_r, _, d = xbuf.shape[1:]
    rows = n_r * SUBLANES
    d_ff = wdown_ref.shape[0]
    slabs_per_block = col_block // LANES
    t = pl.program_id(0)
    n_tiles = pl.num_programs(0)
    slot = lax.rem(t, 2)
    starts_seq = lax.rem(t, tiles_per_seq) == 0

    @pl.when(t == 0)
    def _():
        zbuf_ref[:, rows:rows + HIST, :] = jnp.zeros((zbuf_ref.shape[0], HIST, LANES), _F32)
        for c in _tile_copies(x_hbm, xbuf, in_sem, 0, 0, to_hbm=False):
            c.start()

    @pl.when(t + 1 < n_tiles)
    def _():
        for c in _tile_copies(x_hbm, xbuf, in_sem, t + 1, 1 - slot, to_hbm=False):
            c.start()

    for c in _tile_copies(x_hbm, xbuf, in_sem, t, slot, to_hbm=False):
        c.wait()

    @pl.when(t >= 2)
    def _():
        for c in _tile_copies(o_hbm, obuf, out_sem, t - 2, slot, to_hbm=True):
            c.wait()

    x = xbuf[slot].reshape(rows, d)
    h_ref[...] = _rmsnorm(x, g_ref[...]).astype(_BF16)
    prev = zbuf_ref[:, rows:rows + HIST, :]
    pbuf_ref[...] = jnp.where(starts_seq, jnp.zeros_like(prev), prev)
    for j in range(d_ff // col_block):
        for c0 in (j * col_block, d_ff + j * col_block):
            up = _dot(h_ref[...], wup_ref[:, c0:c0 + col_block])
            _store_slabs(zbuf_ref, c0 // LANES, HIST, rows, up)
            for k in range(slabs_per_block):
                _fill_history(zbuf_ref, pbuf_ref, rows, c0 // LANES + k)
        for k in range(slabs_per_block):
            s = j * slabs_per_block + k
            gate = _conv3_groups(zbuf_ref, cw_ref, rows, s)
            val = _conv3_groups(zbuf_ref, cw_ref, rows, d_ff // LANES + s)
            act_ref[:, s * LANES:(s + 1) * LANES] = (gate * jax.nn.sigmoid(gate) * val).astype(_BF16)
    y = x + _dot(act_ref[...], wdown_ref[...])
    if final_norm:
        y = _rmsnorm(y, fg_ref[...])
    obuf[slot] = y.reshape(n_r, SUBLANES, d)

    for c in _tile_copies(o_hbm, obuf, out_sem, t, slot, to_hbm=True):
        c.start()

    @pl.when(t == n_tiles - 1)
    def _():
        for c in _tile_copies(o_hbm, obuf, out_sem, t, slot, to_hbm=True):
            c.wait()

    @pl.when(jnp.logical_and(t == n_tiles - 1, t >= 1))
    def _():
        for c in _tile_copies(o_hbm, obuf, out_sem, t - 1, 1 - slot, to_hbm=True):
            c.wait()


def _layer(stacked, layer):
    index = (layer,) + (0,) * (stacked.ndim - 1)
    return pl.BlockSpec((None,) + stacked.shape[1:], lambda *_: index, pipeline_mode=pl.Buffered(1))


def _whole(array):
    index = (0,) * array.ndim
    return pl.BlockSpec(array.shape, lambda *_: index, pipeline_mode=pl.Buffered(1))


def _with_casts(body, n_inputs, n_casts):
    def kernel_with_casts(*refs):
        inputs, refs = refs[:n_inputs], refs[n_inputs:]
        cast_src, refs = refs[:n_casts], refs[n_casts:]
        out, refs = refs[0], refs[1:]
        cast_dst, scratch = refs[:n_casts], refs[n_casts:]
        for src, dst in zip(cast_src, cast_dst):
            dst[...] = src[...].astype(_BF16)
        body(*inputs, out, *scratch)
    return kernel_with_casts


def _cast_specs(stacked, layer, n_steps, step_of):
    _, k, n = stacked.shape
    block = next(b for b in range(BF16_SUBLANES, k + 1, BF16_SUBLANES)
                 if k % b == 0 and k // b <= n_steps)
    last = k // block - 1
    src = pl.BlockSpec((None, block, n), lambda *ids: (layer, jnp.minimum(step_of(*ids), last), 0))
    dst = pl.BlockSpec((block, n), lambda *ids: (jnp.minimum(step_of(*ids), last), 0))
    return src, dst, jax.ShapeDtypeStruct((k, n), _BF16)


def _mixer_call(x, layer, g, w_in, conv_w, ln_g, ln_b, w_s, bias, w_out, *, cast, cast_layer):
    bsz, seq, d = x.shape
    rows = MIXER_TILE_ROWS
    n_s = seq // rows
    tile = pl.BlockSpec((None, rows, d), lambda b, s: (b, s, 0))
    stacked = lambda p: _layer(p, layer)
    in_specs = [tile, stacked(g), _whole(w_in), stacked(conv_w), stacked(ln_g), stacked(ln_b),
                stacked(w_s), stacked(bias), _whole(w_out)]
    casts = [_cast_specs(w, cast_layer, bsz * n_s, lambda b, s: b * n_s + s) for w in cast]
    outs = pl.pallas_call(
        _with_casts(_mixer_kernel, len(in_specs), len(casts)),
        grid=(bsz, n_s),
        in_specs=in_specs + [c[0] for c in casts],
        out_specs=[tile] + [c[1] for c in casts],
        out_shape=[jax.ShapeDtypeStruct(x.shape, x.dtype)] + [c[2] for c in casts],
        scratch_shapes=[
            pltpu.VMEM((rows, d), _BF16),
            pltpu.VMEM((d // LANES, SUBLANES + rows, LANES), _F32),
            pltpu.VMEM((rows, d), _BF16),
            pltpu.VMEM((rows, d), _F32),
            pltpu.VMEM((rows, d), _F32),
        ],
        compiler_params=pltpu.CompilerParams(
            dimension_semantics=("arbitrary", "arbitrary"),
            vmem_limit_bytes=VMEM_LIMIT_BYTES),
        name="token_mixer",
    )(x, g, w_in, conv_w, ln_g, ln_b, w_s, bias, w_out, *cast)
    return outs[0], outs[1:]


def _ffn_call(x, layer, g, w_up, conv_w, w_down, final_g, *, final_norm, cast, cast_layer):
    bsz, seq, d = x.shape
    d_ff = w_down.shape[0]
    rows = TILE_ROWS
    n_r = rows // SUBLANES
    n_slabs = 2 * d_ff // LANES
    n_tiles = bsz * seq // rows
    x2 = x.reshape(bsz * seq, d)
    stacked = lambda p: _layer(p, layer)
    in_specs = [pl.BlockSpec(memory_space=pl.ANY), stacked(g), _whole(w_up), stacked(conv_w),
                _whole(w_down), _layer(final_g, 0)]
    casts = [_cast_specs(w, cast_layer, n_tiles, lambda t: t) for w in cast]
    body = functools.partial(_ffn_kernel, final_norm=final_norm, col_block=256,
                             tiles_per_seq=seq // rows)
    outs = pl.pallas_call(
        _with_casts(body, len(in_specs), len(casts)),
        grid=(n_tiles,),
        in_specs=in_specs + [c[0] for c in casts],
        out_specs=[pl.BlockSpec(memory_space=pl.ANY)] + [c[1] for c in casts],
        out_shape=[jax.ShapeDtypeStruct(x2.shape, x.dtype)] + [c[2] for c in casts],
        scratch_shapes=[
            pltpu.VMEM((2, n_r, SUBLANES, d), _F32),
            pltpu.VMEM((2, n_r, SUBLANES, d), _F32),
            pltpu.SemaphoreType.DMA((2, SUBLANES)),
            pltpu.SemaphoreType.DMA((2, SUBLANES)),
            pltpu.VMEM((rows, d), _BF16),
            pltpu.VMEM((n_slabs, HIST + rows, LANES), _F32),
            pltpu.VMEM((n_slabs, HIST, LANES), _F32),
            pltpu.VMEM((rows, d_ff), _BF16),
        ],
        compiler_params=pltpu.CompilerParams(
            dimension_semantics=("arbitrary",),
            vmem_limit_bytes=VMEM_LIMIT_BYTES),
        name="channel_mixer",
    )(x2, g, w_up, conv_w, w_down, final_g, *cast)
    return outs[0].reshape(x.shape), outs[1:]


def kernel(x, mix_norm_g, w_in, conv_a_w, ln_v_g, ln_v_b, w_s, b_s, w_out, ffn_norm_g, w_up,
           conv_ffn_w, w_down, final_norm_g):
    depth, d = mix_norm_g.shape
    assert x.shape[1] % TILE_ROWS == 0 and x.shape[1] % MIXER_TILE_ROWS == 0
    assert MIXER_TILE_ROWS % (2 * CHUNK) == 0
    gdim = d // SGU_GROUPS
    rows_of = lambda a: a.reshape(a.shape[0], 1, a.shape[1])
    bias = jnp.repeat(jnp.swapaxes(b_s, 1, 2), gdim, axis=-1)
    final_g = final_norm_g.reshape(1, 1, d)
    wb_in, wb_out = w_in[0].astype(_BF16), w_out[0].astype(_BF16)
    for l in range(depth):
        x, (wb_up, wb_down) = _mixer_call(
            x, l, rows_of(mix_norm_g), wb_in, conv_a_w, rows_of(ln_v_g), rows_of(ln_v_b), w_s, bias,
            wb_out, cast=(w_up, w_down), cast_layer=l)
        last = l == depth - 1
        x, next_weights = _ffn_call(
            x, l, rows_of(ffn_norm_g), wb_up, conv_ffn_w, wb_down, final_g, final_norm=last,
            cast=() if last else (w_in, w_out), cast_layer=l + 1)
        if not last:
            wb_in, wb_out = next_weights
    return x
```

```python
import functools

import jax
import jax.numpy as jnp
from jax import lax
from jax.experimental import pallas as pl
from jax.experimental.pallas import tpu as pltpu

EPS = 1e-6
CHUNK = 128
SGU_GROUPS = 8
CONV_WIDTH = 3
SUBLANES = 8
LANES = 128
BF16_SUBLANES = 16
TILE_ROWS = 512
MIXER_TILE_ROWS = 1024
VMEM_LIMIT_BYTES = 56 * 1024 * 1024

_BF16 = jnp.bfloat16
_F32 = jnp.float32


def _rmsnorm(x, g):
    ms = jnp.mean(x * x, axis=-1, keepdims=True)
    return x * lax.rsqrt(ms + EPS) * g


def _dot(a, b):
    return jnp.dot(a, b, preferred_element_type=_F32)


def _store_slabs(zbuf_ref, first_slab, row0, rows, value):
    for k in range(value.shape[1] // LANES):
        zbuf_ref[first_slab + k, row0:row0 + rows, :] = value[:, k * LANES:(k + 1) * LANES]


def _conv3_offset(zbuf_ref, w_ref, rows, slab):
    w = w_ref[:, slab * LANES:(slab + 1) * LANES]
    z2 = zbuf_ref[slab, SUBLANES - 2:SUBLANES - 2 + rows, :]
    z1 = zbuf_ref[slab, SUBLANES - 1:SUBLANES - 1 + rows, :]
    z0 = zbuf_ref[slab, SUBLANES:SUBLANES + rows, :]
    return z2 * w[0:1, :] + z1 * w[1:2, :] + z0 * w[2:3, :]


def _mixer_kernel(x_ref, g_ref, win_ref, cw_ref, lng_ref, lnb_ref, ws_ref, bias_ref, wout_ref,
                  o_ref, h_ref, zbuf_ref, vn_ref, mixed_ref, merged_ref):
    rows, d = x_ref.shape
    first_tile = pl.program_id(1) == 0
    col = lambda k: slice(k * d, (k + 1) * d)

    h_ref[...] = _rmsnorm(x_ref[...], g_ref[...]).astype(_BF16)
    proj = lambda k: _dot(h_ref[...], win_ref[:, col(k)])

    v = proj(4)
    mu = jnp.mean(v, axis=-1, keepdims=True)
    vc = v - mu
    var = jnp.mean(vc * vc, axis=-1, keepdims=True)
    vn_ref[...] = (vc * lax.rsqrt(var + EPS) * lng_ref[...] + lnb_ref[...]).astype(_BF16)

    prev = zbuf_ref[:, rows:rows + SUBLANES, :]
    zbuf_ref[:, 0:SUBLANES, :] = jnp.where(first_tile, jnp.zeros_like(prev), prev)
    _store_slabs(zbuf_ref, 0, SUBLANES, rows, proj(1) * proj(2))
    conv = jnp.concatenate([_conv3_offset(zbuf_ref, cw_ref, rows, s) for s in range(d // LANES)],
                           axis=1)

    t_idx = lax.broadcasted_iota(jnp.int32, (CHUNK, CHUNK), 0)
    s_idx = lax.broadcasted_iota(jnp.int32, (CHUNK, CHUNK), 1)
    causal = s_idx <= t_idx
    gdim = d // SGU_GROUPS
    n_chunks = rows // CHUNK
    for g in range(SGU_GROUPS):
        wg = jnp.where(causal, ws_ref[g], 0.0).astype(_BF16)
        gcols = slice(g * gdim, (g + 1) * gdim)
        for n in range(0, n_chunks, 2):
            r0 = slice(n * CHUNK, (n + 1) * CHUNK)
            r1 = slice((n + 1) * CHUNK, (n + 2) * CHUNK)
            rhs = jnp.concatenate([vn_ref[r0, gcols], vn_ref[r1, gcols]], axis=1)
            res = _dot(wg, rhs)
            mixed_ref[r0, gcols] = res[:, :gdim]
            mixed_ref[r1, gcols] = res[:, gdim:]
    mixed = mixed_ref[...].reshape(n_chunks, CHUNK, d) + bias_ref[...][None]
    y_b = proj(3) * mixed.reshape(rows, d)
    merged_ref[...] = jax.nn.sigmoid(proj(6)) * y_b

    gated_conv = jax.nn.sigmoid(proj(5)) * conv
    merged = merged_ref[...] + proj(0) * gated_conv

    o_ref[...] = x_ref[...] + _dot(merged.astype(_BF16), wout_ref[...])


HIST = 2 * SUBLANES


def _tile_copies(hbm_ref, buf_ref, sem_ref, tile, slot, to_hbm):
    r = buf_ref.shape[1]
    copies = []
    for sub in range(SUBLANES):
        hbm = hbm_ref.at[pl.ds((tile * SUBLANES + sub) * r, r), :]
        vmem = buf_ref.at[slot, :, sub, :]
        src, dst = (vmem, hbm) if to_hbm else (hbm, vmem)
        copies.append(pltpu.make_async_copy(src, dst, sem_ref.at[slot, sub]))
    return copies


def _conv3_groups(zbuf_ref, w_ref, rows, slab):
    w = w_ref[:, slab * LANES:(slab + 1) * LANES]
    z = zbuf_ref[slab]
    z2, z1, z0 = z[0:rows], z[SUBLANES:SUBLANES + rows], z[HIST:HIST + rows]
    return z2 * w[0:1, :] + z1 * w[1:2, :] + z0 * w[2:3, :]


def _fill_history(zbuf_ref, pbuf_ref, rows, slab):
    last = zbuf_ref[slab, rows:rows + HIST, :]
    prev = pbuf_ref[slab]
    is_last_sublane = lax.broadcasted_iota(jnp.int32, (SUBLANES, LANES), 0) == SUBLANES - 1
    for k in range(2):
        rk = slice(k * SUBLANES, (k + 1) * SUBLANES)
        src = jnp.where(is_last_sublane, prev[rk], last[rk])
        zbuf_ref[slab, rk, :] = pltpu.roll(src, 1, 0)


def _ffn_kernel(x_hbm, g_ref, wup_ref, cw_ref, wdown_ref, fg_ref, o_hbm,
                xbuf, obuf, in_sem, out_sem, h_ref, zbuf_ref, pbuf_ref, act_ref,
                *, final_norm, col_block, tiles_per_seq):
    n_r, _, d = xbuf.shape[1:]
    rows = n_r * SUBLANES
    d_ff = wdown_ref.shape[0]
    slabs_per_block = col_block // LANES
    t = pl.program_id(0)
    n_tiles = pl.num_programs(0)
    slot = lax.rem(t, 2)
    starts_seq = lax.rem(t, tiles_per_seq) == 0

    @pl.when(t == 0)
    def _():
        zbuf_ref[:, rows:rows + HIST, :] = jnp.zeros((zbuf_ref.shape[0], HIST, LANES), _F32)
        for c in _tile_copies(x_hbm, xbuf, in_sem, 0, 0, to_hbm=False):
            c.start()

    @pl.when(t + 1 < n_tiles)
    def _():
        for c in _tile_copies(x_hbm, xbuf, in_sem, t + 1, 1 - slot, to_hbm=False):
            c.start()

    for c in _tile_copies(x_hbm, xbuf, in_sem, t, slot, to_hbm=False):
        c.wait()

    @pl.when(t >= 2)
    def _():
        for c in _tile_copies(o_hbm, obuf, out_sem, t - 2, slot, to_hbm=True):
            c.wait()

    x = xbuf[slot].reshape(rows, d)
    h_ref[...] = _rmsnorm(x, g_ref[...]).astype(_BF16)
    prev = zbuf_ref[:, rows:rows + HIST, :]
    pbuf_ref[...] = jnp.where(starts_seq, jnp.zeros_like(prev), prev)
    for j in range(d_ff // col_block):
        for c0 in (j * col_block, d_ff + j * col_block):
            up = _dot(h_ref[...], wup_ref[:, c0:c0 + col_block])
            _store_slabs(zbuf_ref, c0 // LANES, HIST, rows, up)
            for k in range(slabs_per_block):
                _fill_history(zbuf_ref, pbuf_ref, rows, c0 // LANES + k)
        for k in range(slabs_per_block):
            s = j * slabs_per_block + k
            gate = _conv3_groups(zbuf_ref, cw_ref, rows, s)
            val = _conv3_groups(zbuf_ref, cw_ref, rows, d_ff // LANES + s)
            act_ref[:, s * LANES:(s + 1) * LANES] = (gate * jax.nn.sigmoid(gate) * val).astype(_BF16)
    y = x + _dot(act_ref[...], wdown_ref[...])
    if final_norm:
        y = _rmsnorm(y, fg_ref[...])
    obuf[slot] = y.reshape(n_r, SUBLANES, d)

    for c in _tile_copies(o_hbm, obuf, out_sem, t, slot, to_hbm=True):
        c.start()

    @pl.when(t == n_tiles - 1)
    def _():
        for c in _tile_copies(o_hbm, obuf, out_sem, t, slot, to_hbm=True):
            c.wait()

    @pl.when(jnp.logical_and(t == n_tiles - 1, t >= 1))
    def _():
        for c in _tile_copies(o_hbm, obuf, out_sem, t - 1, 1 - slot, to_hbm=True):
            c.wait()


def _layer(stacked, layer):
    index = (layer,) + (0,) * (stacked.ndim - 1)
    return pl.BlockSpec((None,) + stacked.shape[1:], lambda *_: index, pipeline_mode=pl.Buffered(1))


def _whole(array):
    index = (0,) * array.ndim
    return pl.BlockSpec(array.shape, lambda *_: index, pipeline_mode=pl.Buffered(1))


def _with_casts(body, n_inputs, n_casts):
    def kernel_with_casts(*refs):
        inputs, refs = refs[:n_inputs], refs[n_inputs:]
        cast_src, refs = refs[:n_casts], refs[n_casts:]
        out, refs = refs[0], refs[1:]
        cast_dst, scratch = refs[:n_casts], refs[n_casts:]
        for src, dst in zip(cast_src, cast_dst):
            dst[...] = src[...].astype(_BF16)
        body(*inputs, out, *scratch)
    return kernel_with_casts


def _cast_specs(stacked, layer, n_steps, step_of):
    _, k, n = stacked.shape
    block = next(b for b in range(BF16_SUBLANES, k + 1, BF16_SUBLANES)
                 if k % b == 0 and k // b <= n_steps)
    last = k // block - 1
    src = pl.BlockSpec((None, block, n), lambda *ids: (layer, jnp.minimum(step_of(*ids), last), 0))
    dst = pl.BlockSpec((block, n), lambda *ids: (jnp.minimum(step_of(*ids), last), 0))
    return src, dst, jax.ShapeDtypeStruct((k, n), _BF16)


def _mixer_call(x, layer, g, w_in, conv_w, ln_g, ln_b, w_s, bias, w_out, *, cast, cast_layer):
    bsz, seq, d = x.shape
    rows = MIXER_TILE_ROWS
    n_s = seq // rows
    tile = pl.BlockSpec((None, rows, d), lambda b, s: (b, s, 0))
    stacked = lambda p: _layer(p, layer)
    in_specs = [tile, stacked(g), _whole(w_in), stacked(conv_w), stacked(ln_g), stacked(ln_b),
                stacked(w_s), stacked(bias), _whole(w_out)]
    casts = [_cast_specs(w, cast_layer, bsz * n_s, lambda b, s: b * n_s + s) for w in cast]
    outs = pl.pallas_call(
        _with_casts(_mixer_kernel, len(in_specs), len(casts)),
        grid=(bsz, n_s),
        in_specs=in_specs + [c[0] for c in casts],
        out_specs=[tile] + [c[1] for c in casts],
        out_shape=[jax.ShapeDtypeStruct(x.shape, x.dtype)] + [c[2] for c in casts],
        scratch_shapes=[
            pltpu.VMEM((rows, d), _BF16),
            pltpu.VMEM((d // LANES, SUBLANES + rows, LANES), _F32),
            pltpu.VMEM((rows, d), _BF16),
            pltpu.VMEM((rows, d), _F32),
            pltpu.VMEM((rows, d), _F32),
        ],
        compiler_params=pltpu.CompilerParams(
            dimension_semantics=("arbitrary", "arbitrary"),
            vmem_limit_bytes=VMEM_LIMIT_BYTES),
        name="token_mixer",
    )(x, g, w_in, conv_w, ln_g, ln_b, w_s, bias, w_out, *cast)
    return outs[0], outs[1:]


def _ffn_call(x, layer, g, w_up, conv_w, w_down, final_g, *, final_norm, cast, cast_layer):
    bsz, seq, d = x.shape
    d_ff = w_down.shape[0]
    rows = TILE_ROWS
    n_r = rows // SUBLANES
    n_slabs = 2 * d_ff // LANES
    n_tiles = bsz * seq // rows
    x2 = x.reshape(bsz * seq, d)
    stacked = lambda p: _layer(p, layer)
    in_specs = [pl.BlockSpec(memory_space=pl.ANY), stacked(g), _whole(w_up), stacked(conv_w),
                _whole(w_down), _layer(final_g, 0)]
    casts = [_cast_specs(w, cast_layer, n_tiles, lambda t: t) for w in cast]
    body = functools.partial(_ffn_kernel, final_norm=final_norm, col_block=256,
                             tiles_per_seq=seq // rows)
    outs = pl.pallas_call(
        _with_casts(body, len(in_specs), len(casts)),
        grid=(n_tiles,),
        in_specs=in_specs + [c[0] for c in casts],
        out_specs=[pl.BlockSpec(memory_space=pl.ANY)] + [c[1] for c in casts],
        out_shape=[jax.ShapeDtypeStruct(x2.shape, x.dtype)] + [c[2] for c in casts],
        scratch_shapes=[
            pltpu.VMEM((2, n_r, SUBLANES, d), _F32),
            pltpu.VMEM((2, n_r, SUBLANES, d), _F32),
            pltpu.SemaphoreType.DMA((2, SUBLANES)),
            pltpu.SemaphoreType.DMA((2, SUBLANES)),
            pltpu.VMEM((rows, d), _BF16),
            pltpu.VMEM((n_slabs, HIST + rows, LANES), _F32),
            pltpu.VMEM((n_slabs, HIST, LANES), _F32),
            pltpu.VMEM((rows, d_ff), _BF16),
        ],
        compiler_params=pltpu.CompilerParams(
            dimension_semantics=("arbitrary",),
            vmem_limit_bytes=VMEM_LIMIT_BYTES),
        name="channel_mixer",
    )(x2, g, w_up, conv_w, w_down, final_g, *cast)
    return outs[0].reshape(x.shape), outs[1:]


def kernel(x, mix_norm_g, w_in, conv_a_w, ln_v_g, ln_v_b, w_s, b_s, w_out, ffn_norm_g, w_up,
           conv_ffn_w, w_down, final_norm_g):
    depth, d = mix_norm_g.shape
    assert x.shape[1] % TILE_ROWS == 0 and x.shape[1] % MIXER_TILE_ROWS == 0
    assert MIXER_TILE_ROWS % (2 * CHUNK) == 0
    gdim = d // SGU_GROUPS
    rows_of = lambda a: a.reshape(a.shape[0], 1, a.shape[1])
    bias = jnp.repeat(jnp.swapaxes(b_s, 1, 2), gdim, axis=-1)
    final_g = final_norm_g.reshape(1, 1, d)
    wb_in, wb_out = w_in[0].astype(_BF16), w_out[0].astype(_BF16)
    for l in range(depth):
        x, (wb_up, wb_down) = _mixer_call(
            x, l, rows_of(mix_norm_g), wb_in, conv_a_w, rows_of(ln_v_g), rows_of(ln_v_b), w_s, bias,
            wb_out, cast=(w_up, w_down), cast_layer=l)
        last = l == depth - 1
        x, next_weights = _ffn_call(
            x, l, rows_of(ffn_norm_g), wb_up, conv_ffn_w, wb_down, final_g, final_norm=last,
            cast=() if last else (w_in, w_out), cast_layer=l + 1)
        if not last:
            wb_in, wb_out = next_weights
    return x
```

```python
import functools

import jax
import jax.numpy as jnp
from jax import lax
from jax.experimental import pallas as pl
from jax.experimental.pallas import tpu as pltpu

EPS = 1e-6
CHUNK = 128
SGU_GROUPS = 8
CONV_WIDTH = 3
SUBLANES = 8
LANES = 128
BF16_SUBLANES = 16
TILE_ROWS = 1024
MIXER_TILE_ROWS = 1024
VMEM_LIMIT_BYTES = 56 * 1024 * 1024

_BF16 = jnp.bfloat16
_F32 = jnp.float32


def _rmsnorm(x, g):
    ms = jnp.mean(x * x, axis=-1, keepdims=True)
    return x * lax.rsqrt(ms + EPS) * g


def _dot(a, b):
    return jnp.dot(a, b, preferred_element_type=_F32)


def _store_slabs(zbuf_ref, first_slab, row0, rows, value):
    for k in range(value.shape[1] // LANES):
        zbuf_ref[first_slab + k, row0:row0 + rows, :] = value[:, k * LANES:(k + 1) * LANES]


def _conv3_offset(zbuf_ref, w_ref, rows, slab):
    w = w_ref[:, slab * LANES:(slab + 1) * LANES]
    z2 = zbuf_ref[slab, SUBLANES - 2:SUBLANES - 2 + rows, :]
    z1 = zbuf_ref[slab, SUBLANES - 1:SUBLANES - 1 + rows, :]
    z0 = zbuf_ref[slab, SUBLANES:SUBLANES + rows, :]
    return z2 * w[0:1, :] + z1 * w[1:2, :] + z0 * w[2:3, :]


def _mixer_kernel(x_ref, g_ref, win_ref, cw_ref, lng_ref, lnb_ref, ws_ref, bias_ref, wout_ref,
                  o_ref, h_ref, zbuf_ref, vn_ref, mixed_ref, merged_ref):
    rows, d = x_ref.shape
    first_tile = pl.program_id(1) == 0
    col = lambda k: slice(k * d, (k + 1) * d)

    h_ref[...] = _rmsnorm(x_ref[...], g_ref[...]).astype(_BF16)
    proj = lambda k: _dot(h_ref[...], win_ref[:, col(k)])

    v = proj(4)
    mu = jnp.mean(v, axis=-1, keepdims=True)
    vc = v - mu
    var = jnp.mean(vc * vc, axis=-1, keepdims=True)
    vn_ref[...] = (vc * lax.rsqrt(var + EPS) * lng_ref[...] + lnb_ref[...]).astype(_BF16)

    prev = zbuf_ref[:, rows:rows + SUBLANES, :]
    zbuf_ref[:, 0:SUBLANES, :] = jnp.where(first_tile, jnp.zeros_like(prev), prev)
    _store_slabs(zbuf_ref, 0, SUBLANES, rows, proj(1) * proj(2))
    conv = jnp.concatenate([_conv3_offset(zbuf_ref, cw_ref, rows, s) for s in range(d // LANES)],
                           axis=1)

    t_idx = lax.broadcasted_iota(jnp.int32, (CHUNK, CHUNK), 0)
    s_idx = lax.broadcasted_iota(jnp.int32, (CHUNK, CHUNK), 1)
    causal = s_idx <= t_idx
    gdim = d // SGU_GROUPS
    n_chunks = rows // CHUNK
    for g in range(SGU_GROUPS):
        wg = jnp.where(causal, ws_ref[g], 0.0).astype(_BF16)
        gcols = slice(g * gdim, (g + 1) * gdim)
        for n in range(0, n_chunks, 2):
            r0 = slice(n * CHUNK, (n + 1) * CHUNK)
            r1 = slice((n + 1) * CHUNK, (n + 2) * CHUNK)
            rhs = jnp.concatenate([vn_ref[r0, gcols], vn_ref[r1, gcols]], axis=1)
            res = _dot(wg, rhs)
            mixed_ref[r0, gcols] = res[:, :gdim]
            mixed_ref[r1, gcols] = res[:, gdim:]
    mixed = mixed_ref[...].reshape(n_chunks, CHUNK, d) + bias_ref[...][None]
    y_b = proj(3) * mixed.reshape(rows, d)
    merged_ref[...] = jax.nn.sigmoid(proj(6)) * y_b

    gated_conv = jax.nn.sigmoid(proj(5)) * conv
    merged = merged_ref[...] + proj(0) * gated_conv

    o_ref[...] = x_ref[...] + _dot(merged.astype(_BF16), wout_ref[...])


HIST = 2 * SUBLANES


def _tile_copies(hbm_ref, buf_ref, sem_ref, tile, slot, to_hbm):
    r = buf_ref.shape[1]
    copies = []
    for sub in range(SUBLANES):
        hbm = hbm_ref.at[pl.ds((tile * SUBLANES + sub) * r, r), :]
        vmem = buf_ref.at[slot, :, sub, :]
        src, dst = (vmem, hbm) if to_hbm else (hbm, vmem)
        copies.append(pltpu.make_async_copy(src, dst, sem_ref.at[slot, sub]))
    return copies


def _conv3_groups(u, before, w):
    rows = u.shape[0]
    z = jnp.concatenate([before, u], axis=0)
    z2, z1, z0 = z[0:rows], z[SUBLANES:SUBLANES + rows], z[HIST:HIST + rows]
    return z2 * w[0:1, :] + z1 * w[1:2, :] + z0 * w[2:3, :]


def _first_group_operands(u, pbuf_ref, slab, starts_seq):
    rows = u.shape[0]
    last = u[rows - HIST:rows]
    prev = pbuf_ref[slab]
    prev = jnp.where(starts_seq, jnp.zeros_like(prev), prev)
    pbuf_ref[slab] = last
    is_last_sublane = lax.broadcasted_iota(jnp.int32, (SUBLANES, LANES), 0) == SUBLANES - 1
    groups = []
    for k in range(2):
        rk = slice(k * SUBLANES, (k + 1) * SUBLANES)
        groups.append(pltpu.roll(jnp.where(is_last_sublane, prev[rk], last[rk]), 1, 0))
    return jnp.concatenate(groups, axis=0)


def _ffn_kernel(x_hbm, g_ref, wup_ref, cw_ref, wdown_ref, fg_ref, o_hbm,
                xbuf, obuf, in_sem, out_sem, h_ref, pbuf_ref, act_ref,
                *, final_norm, col_block, tiles_per_seq):
    n_r, _, d = xbuf.shape[1:]
    rows = n_r * SUBLANES
    d_ff = wdown_ref.shape[0]
    t = pl.program_id(0)
    n_tiles = pl.num_programs(0)
    slot = lax.rem(t, 2)
    starts_seq = lax.rem(t, tiles_per_seq) == 0

    @pl.when(t == 0)
    def _():
        pbuf_ref[...] = jnp.zeros_like(pbuf_ref)
        for c in _tile_copies(x_hbm, xbuf, in_sem, 0, 0, to_hbm=False):
            c.start()

    @pl.when(t + 1 < n_tiles)
    def _():
        for c in _tile_copies(x_hbm, xbuf, in_sem, t + 1, 1 - slot, to_hbm=False):
            c.start()

    for c in _tile_copies(x_hbm, xbuf, in_sem, t, slot, to_hbm=False):
        c.wait()

    @pl.when(t >= 2)
    def _():
        for c in _tile_copies(o_hbm, obuf, out_sem, t - 2, slot, to_hbm=True):
            c.wait()

    x = xbuf[slot].reshape(rows, d)
    h_ref[...] = _rmsnorm(x, g_ref[...]).astype(_BF16)
    n_gate_slabs = d_ff // LANES
    for j in range(d_ff // col_block):
        gate_up = _dot(h_ref[...], wup_ref[:, j * col_block:(j + 1) * col_block])
        val_up = _dot(h_ref[...], wup_ref[:, d_ff + j * col_block:d_ff + (j + 1) * col_block])
        for k in range(col_block // LANES):
            s = j * (col_block // LANES) + k
            conv = []
            for up, slab in ((gate_up, s), (val_up, n_gate_slabs + s)):
                u = up[:, k * LANES:(k + 1) * LANES]
                before = _first_group_operands(u, pbuf_ref, slab, starts_seq)
                conv.append(_conv3_groups(u, before, cw_ref[:, slab * LANES:(slab + 1) * LANES]))
            gate, val = conv
            act_ref[:, s * LANES:(s + 1) * LANES] = (gate * jax.nn.sigmoid(gate) * val).astype(_BF16)
    y = x + _dot(act_ref[...], wdown_ref[...])
    if final_norm:
        y = _rmsnorm(y, fg_ref[...])
    obuf[slot] = y.reshape(n_r, SUBLANES, d)

    for c in _tile_copies(o_hbm, obuf, out_sem, t, slot, to_hbm=True):
        c.start()

    @pl.when(t == n_tiles - 1)
    def _():
        for c in _tile_copies(o_hbm, obuf, out_sem, t, slot, to_hbm=True):
            c.wait()

    @pl.when(jnp.logical_and(t == n_tiles - 1, t >= 1))
    def _():
        for c in _tile_copies(o_hbm, obuf, out_sem, t - 1, 1 - slot, to_hbm=True):
            c.wait()


def _layer(stacked, layer):
    index = (layer,) + (0,) * (stacked.ndim - 1)
    return pl.BlockSpec((None,) + stacked.shape[1:], lambda *_: index, pipeline_mode=pl.Buffered(1))


def _whole(array):
    index = (0,) * array.ndim
    return pl.BlockSpec(array.shape, lambda *_: index, pipeline_mode=pl.Buffered(1))


def _with_casts(body, n_inputs, n_casts):
    def kernel_with_casts(*refs):
        inputs, refs = refs[:n_inputs], refs[n_inputs:]
        cast_src, refs = refs[:n_casts], refs[n_casts:]
        out, refs = refs[0], refs[1:]
        cast_dst, scratch = refs[:n_casts], refs[n_casts:]
        for src, dst in zip(cast_src, cast_dst):
            dst[...] = src[...].astype(_BF16)
        body(*inputs, out, *scratch)
    return kernel_with_casts


def _cast_specs(stacked, layer, n_steps, step_of):
    _, k, n = stacked.shape
    block = next(b for b in range(BF16_SUBLANES, k + 1, BF16_SUBLANES)
                 if k % b == 0 and k // b <= n_steps)
    last = k // block - 1
    src = pl.BlockSpec((None, block, n), lambda *ids: (layer, jnp.minimum(step_of(*ids), last), 0))
    dst = pl.BlockSpec((block, n), lambda *ids: (jnp.minimum(step_of(*ids), last), 0))
    return src, dst, jax.ShapeDtypeStruct((k, n), _BF16)


def _mixer_call(x, layer, g, w_in, conv_w, ln_g, ln_b, w_s, bias, w_out, *, cast, cast_layer):
    bsz, seq, d = x.shape
    rows = MIXER_TILE_ROWS
    n_s = seq // rows
    tile = pl.BlockSpec((None, rows, d), lambda b, s: (b, s, 0))
    stacked = lambda p: _layer(p, layer)
    in_specs = [tile, stacked(g), _whole(w_in), stacked(conv_w), stacked(ln_g), stacked(ln_b),
                stacked(w_s), stacked(bias), _whole(w_out)]
    casts = [_cast_specs(w, cast_layer, bsz * n_s, lambda b, s: b * n_s + s) for w in cast]
    outs = pl.pallas_call(
        _with_casts(_mixer_kernel, len(in_specs), len(casts)),
        grid=(bsz, n_s),
        in_specs=in_specs + [c[0] for c in casts],
        out_specs=[tile] + [c[1] for c in casts],
        out_shape=[jax.ShapeDtypeStruct(x.shape, x.dtype)] + [c[2] for c in casts],
        scratch_shapes=[
            pltpu.VMEM((rows, d), _BF16),
            pltpu.VMEM((d // LANES, SUBLANES + rows, LANES), _F32),
            pltpu.VMEM((rows, d), _BF16),
            pltpu.VMEM((rows, d), _F32),
            pltpu.VMEM((rows, d), _F32),
        ],
        compiler_params=pltpu.CompilerParams(
            dimension_semantics=("arbitrary", "arbitrary"),
            vmem_limit_bytes=VMEM_LIMIT_BYTES),
        name="token_mixer",
    )(x, g, w_in, conv_w, ln_g, ln_b, w_s, bias, w_out, *cast)
    return outs[0], outs[1:]


def _ffn_call(x, layer, g, w_up, conv_w, w_down, final_g, *, final_norm, cast, cast_layer):
    bsz, seq, d = x.shape
    d_ff = w_down.shape[0]
    rows = TILE_ROWS
    n_r = rows // SUBLANES
    n_slabs = 2 * d_ff // LANES
    n_tiles = bsz * seq // rows
    x2 = x.reshape(bsz * seq, d)
    stacked = lambda p: _layer(p, layer)
    in_specs = [pl.BlockSpec(memory_space=pl.ANY), stacked(g), _whole(w_up), stacked(conv_w),
                _whole(w_down), _layer(final_g, 0)]
    casts = [_cast_specs(w, cast_layer, n_tiles, lambda t: t) for w in cast]
    body = functools.partial(_ffn_kernel, final_norm=final_norm, col_block=256,
                             tiles_per_seq=seq // rows)
    outs = pl.pallas_call(
        _with_casts(body, len(in_specs), len(casts)),
        grid=(n_tiles,),
        in_specs=in_specs + [c[0] for c in casts],
        out_specs=[pl.BlockSpec(memory_space=pl.ANY)] + [c[1] for c in casts],
        out_shape=[jax.ShapeDtypeStruct(x2.shape, x.dtype)] + [c[2] for c in casts],
        scratch_shapes=[
            pltpu.VMEM((2, n_r, SUBLANES, d), _F32),
            pltpu.VMEM((2, n_r, SUBLANES, d), _F32),
            pltpu.SemaphoreType.DMA((2, SUBLANES)),
            pltpu.SemaphoreType.DMA((2, SUBLANES)),
            pltpu.VMEM((rows, d), _BF16),
            pltpu.VMEM((n_slabs, HIST, LANES), _F32),
            pltpu.VMEM((rows, d_ff), _BF16),
        ],
        compiler_params=pltpu.CompilerParams(
            dimension_semantics=("arbitrary",),
            vmem_limit_bytes=VMEM_LIMIT_BYTES),
        name="channel_mixer",
    )(x2, g, w_up, conv_w, w_down, final_g, *cast)
    return outs[0].reshape(x.shape), outs[1:]


def kernel(x, mix_norm_g, w_in, conv_a_w, ln_v_g, ln_v_b, w_s, b_s, w_out, ffn_norm_g, w_up,
           conv_ffn_w, w_down, final_norm_g):
    depth, d = mix_norm_g.shape
    assert x.shape[1] % TILE_ROWS == 0 and x.shape[1] % MIXER_TILE_ROWS == 0
    assert MIXER_TILE_ROWS % (2 * CHUNK) == 0
    gdim = d // SGU_GROUPS
    rows_of = lambda a: a.reshape(a.shape[0], 1, a.shape[1])
    bias = jnp.repeat(jnp.swapaxes(b_s, 1, 2), gdim, axis=-1)
    final_g = final_norm_g.reshape(1, 1, d)
    wb_in, wb_out = w_in[0].astype(_BF16), w_out[0].astype(_BF16)
    for l in range(depth):
        x, (wb_up, wb_down) = _mixer_call(
            x, l, rows_of(mix_norm_g), wb_in, conv_a_w, rows_of(ln_v_g), rows_of(ln_v_b), w_s, bias,
            wb_out, cast=(w_up, w_down), cast_layer=l)
        last = l == depth - 1
        x, next_weights = _ffn_call(
            x, l, rows_of(ffn_norm_g), wb_up, conv_ffn_w, wb_down, final_g, final_norm=last,
            cast=() if last else (w_in, w_out), cast_layer=l + 1)
        if not last:
            wb_in, wb_out = next_weights
    return x
```

```python
import functools

import jax
import jax.numpy as jnp
from jax import lax
from jax.experimental import pallas as pl
from jax.experimental.pallas import tpu as pltpu

EPS = 1e-6
CHUNK = 128
SGU_GROUPS = 8
CONV_WIDTH = 3
SUBLANES = 8
LANES = 128
BF16_SUBLANES = 16
TILE_ROWS = 512
MIXER_TILE_ROWS = 1024
VMEM_LIMIT_BYTES = 56 * 1024 * 1024

_BF16 = jnp.bfloat16
_F32 = jnp.float32


def _rmsnorm(x, g):
    ms = jnp.mean(x * x, axis=-1, keepdims=True)
    return x * lax.rsqrt(ms + EPS) * g


def _dot(a, b):
    return jnp.dot(a, b, preferred_element_type=_F32)


def _store_slabs(zbuf_ref, first_slab, row0, rows, value):
    for k in range(value.shape[1] // LANES):
        zbuf_ref[first_slab + k, row0:row0 + rows, :] = value[:, k * LANES:(k + 1) * LANES]


def _conv3_offset(zbuf_ref, w_ref, rows, slab):
    w = w_ref[:, slab * LANES:(slab + 1) * LANES]
    z2 = zbuf_ref[slab, SUBLANES - 2:SUBLANES - 2 + rows, :]
    z1 = zbuf_ref[slab, SUBLANES - 1:SUBLANES - 1 + rows, :]
    z0 = zbuf_ref[slab, SUBLANES:SUBLANES + rows, :]
    return z2 * w[0:1, :] + z1 * w[1:2, :] + z0 * w[2:3, :]


def _mixer_kernel(x_ref, g_ref, win_ref, cw_ref, lng_ref, lnb_ref, ws_ref, bias_ref, wout_ref,
                  o_ref, h_ref, zbuf_ref, vn_ref, mixed_ref, merged_ref):
    rows, d = x_ref.shape
    first_tile = pl.program_id(1) == 0
    col = lambda k: slice(k * d, (k + 1) * d)

    h_ref[...] = _rmsnorm(x_ref[...], g_ref[...]).astype(_BF16)
    proj = lambda k: _dot(h_ref[...], win_ref[:, col(k)])

    v = proj(4)
    mu = jnp.mean(v, axis=-1, keepdims=True)
    vc = v - mu
    var = jnp.mean(vc * vc, axis=-1, keepdims=True)
    vn_ref[...] = (vc * lax.rsqrt(var + EPS) * lng_ref[...] + lnb_ref[...]).astype(_BF16)

    prev = zbuf_ref[:, rows:rows + SUBLANES, :]
    zbuf_ref[:, 0:SUBLANES, :] = jnp.where(first_tile, jnp.zeros_like(prev), prev)
    _store_slabs(zbuf_ref, 0, SUBLANES, rows, proj(1) * proj(2))
    conv = jnp.concatenate([_conv3_offset(zbuf_ref, cw_ref, rows, s) for s in range(d // LANES)],
                           axis=1)

    t_idx = lax.broadcasted_iota(jnp.int32, (CHUNK, CHUNK), 0)
    s_idx = lax.broadcasted_iota(jnp.int32, (CHUNK, CHUNK), 1)
    causal = s_idx <= t_idx
    gdim = d // SGU_GROUPS
    n_chunks = rows // CHUNK
    for g in range(SGU_GROUPS):
        wg = jnp.where(causal, ws_ref[g], 0.0).astype(_BF16)
        gcols = slice(g * gdim, (g + 1) * gdim)
        for n in range(0, n_chunks, 2):
            r0 = slice(n * CHUNK, (n + 1) * CHUNK)
            r1 = slice((n + 1) * CHUNK, (n + 2) * CHUNK)
            rhs = jnp.concatenate([vn_ref[r0, gcols], vn_ref[r1, gcols]], axis=1)
            res = _dot(wg, rhs)
            mixed_ref[r0, gcols] = res[:, :gdim]
            mixed_ref[r1, gcols] = res[:, gdim:]
    mixed = mixed_ref[...].reshape(n_chunks, CHUNK, d) + bias_ref[...][None]
    y_b = proj(3) * mixed.reshape(rows, d)
    merged_ref[...] = jax.nn.sigmoid(proj(6)) * y_b

    gated_conv = jax.nn.sigmoid(proj(5)) * conv
    merged = merged_ref[...] + proj(0) * gated_conv

    o_ref[...] = x_ref[...] + _dot(merged.astype(_BF16), wout_ref[...])


HIST = 2 * SUBLANES


def _tile_copies(hbm_ref, buf_ref, sem_ref, tile, slot, to_hbm):
    r = buf_ref.shape[1]
    copies = []
    for sub in range(SUBLANES):
        hbm = hbm_ref.at[pl.ds((tile * SUBLANES + sub) * r, r), :]
        vmem = buf_ref.at[slot, :, sub, :]
        src, dst = (vmem, hbm) if to_hbm else (hbm, vmem)
        copies.append(pltpu.make_async_copy(src, dst, sem_ref.at[slot, sub]))
    return copies


def _conv3_groups(u, before, w):
    rows = u.shape[0]
    z = jnp.concatenate([before, u], axis=0)
    z2, z1, z0 = z[0:rows], z[SUBLANES:SUBLANES + rows], z[HIST:HIST + rows]
    return z2 * w[0:1, :] + z1 * w[1:2, :] + z0 * w[2:3, :]


def _first_group_operands(u, pbuf_ref, slab, starts_seq):
    rows = u.shape[0]
    last = u[rows - HIST:rows]
    prev = pbuf_ref[slab]
    prev = jnp.where(starts_seq, jnp.zeros_like(prev), prev)
    pbuf_ref[slab] = last
    is_last_sublane = lax.broadcasted_iota(jnp.int32, (SUBLANES, LANES), 0) == SUBLANES - 1
    groups = []
    for k in range(2):
        rk = slice(k * SUBLANES, (k + 1) * SUBLANES)
        groups.append(pltpu.roll(jnp.where(is_last_sublane, prev[rk], last[rk]), 1, 0))
    return jnp.concatenate(groups, axis=0)


def _ffn_kernel(x_hbm, g_ref, wup_ref, cw_ref, wdown_ref, fg_ref, o_hbm,
                xbuf, obuf, in_sem, out_sem, h_ref, pbuf_ref, act_ref,
                *, final_norm, col_block, tiles_per_seq):
    n_r, _, d = xbuf.shape[1:]
    rows = n_r * SUBLANES
    d_ff = wdown_ref.shape[0]
    t = pl.program_id(0)
    n_tiles = pl.num_programs(0)
    slot = lax.rem(t, 2)
    starts_seq = lax.rem(t, tiles_per_seq) == 0

    @pl.when(t == 0)
    def _():
        pbuf_ref[...] = jnp.zeros_like(pbuf_ref)
        for c in _tile_copies(x_hbm, xbuf, in_sem, 0, 0, to_hbm=False):
            c.start()

    @pl.when(t + 1 < n_tiles)
    def _():
        for c in _tile_copies(x_hbm, xbuf, in_sem, t + 1, 1 - slot, to_hbm=False):
            c.start()

    for c in _tile_copies(x_hbm, xbuf, in_sem, t, slot, to_hbm=False):
        c.wait()

    @pl.when(t >= 2)
    def _():
        for c in _tile_copies(o_hbm, obuf, out_sem, t - 2, slot, to_hbm=True):
            c.wait()

    x = xbuf[slot].reshape(rows, d)
    h_ref[...] = _rmsnorm(x, g_ref[...]).astype(_BF16)
    n_gate_slabs = d_ff // LANES
    for j in range(d_ff // col_block):
        gate_up = _dot(h_ref[...], wup_ref[:, j * col_block:(j + 1) * col_block])
        val_up = _dot(h_ref[...], wup_ref[:, d_ff + j * col_block:d_ff + (j + 1) * col_block])
        for k in range(col_block // LANES):
            s = j * (col_block // LANES) + k
            conv = []
            for up, slab in ((gate_up, s), (val_up, n_gate_slabs + s)):
                u = up[:, k * LANES:(k + 1) * LANES]
                before = _first_group_operands(u, pbuf_ref, slab, starts_seq)
                conv.append(_conv3_groups(u, before, cw_ref[:, slab * LANES:(slab + 1) * LANES]))
            gate, val = conv
            act_ref[:, s * LANES:(s + 1) * LANES] = (gate * jax.nn.sigmoid(gate) * val).astype(_BF16)
    y = x + _dot(act_ref[...], wdown_ref[...])
    if final_norm:
        y = _rmsnorm(y, fg_ref[...])
    obuf[slot] = y.reshape(n_r, SUBLANES, d)

    for c in _tile_copies(o_hbm, obuf, out_sem, t, slot, to_hbm=True):
        c.start()

    @pl.when(t == n_tiles - 1)
    def _():
        for c in _tile_copies(o_hbm, obuf, out_sem, t, slot, to_hbm=True):
            c.wait()

    @pl.when(jnp.logical_and(t == n_tiles - 1, t >= 1))
    def _():
        for c in _tile_copies(o_hbm, obuf, out_sem, t - 1, 1 - slot, to_hbm=True):
            c.wait()


def _layer(stacked, layer):
    index = (layer,) + (0,) * (stacked.ndim - 1)
    return pl.BlockSpec((None,) + stacked.shape[1:], lambda *_: index, pipeline_mode=pl.Buffered(1))


def _whole(array):
    index = (0,) * array.ndim
    return pl.BlockSpec(array.shape, lambda *_: index, pipeline_mode=pl.Buffered(1))


def _with_casts(body, n_inputs, n_casts):
    def kernel_with_casts(*refs):
        inputs, refs = refs[:n_inputs], refs[n_inputs:]
        cast_src, refs = refs[:n_casts], refs[n_casts:]
        out, refs = refs[0], refs[1:]
        cast_dst, scratch = refs[:n_casts], refs[n_casts:]
        for src, dst in zip(cast_src, cast_dst):
            dst[...] = src[...].astype(_BF16)
        body(*inputs, out, *scratch)
    return kernel_with_casts


def _cast_specs(stacked, layer, n_steps, step_of):
    _, k, n = stacked.shape
    block = next(b for b in range(BF16_SUBLANES, k + 1, BF16_SUBLANES)
                 if k % b == 0 and k // b <= n_steps)
    last = k // block - 1
    src = pl.BlockSpec((None, block, n), lambda *ids: (layer, jnp.minimum(step_of(*ids), last), 0))
    dst = pl.BlockSpec((block, n), lambda *ids: (jnp.minimum(step_of(*ids), last), 0))
    return src, dst, jax.ShapeDtypeStruct((k, n), _BF16)


def _mixer_call(x, layer, g, w_in, conv_w, ln_g, ln_b, w_s, bias, w_out, *, cast, cast_layer):
    bsz, seq, d = x.shape
    rows = MIXER_TILE_ROWS
    n_s = seq // rows
    tile = pl.BlockSpec((None, rows, d), lambda b, s: (b, s, 0))
    stacked = lambda p: _layer(p, layer)
    in_specs = [tile, stacked(g), _whole(w_in), stacked(conv_w), stacked(ln_g), stacked(ln_b),
                stacked(w_s), stacked(bias), _whole(w_out)]
    casts = [_cast_specs(w, cast_layer, bsz * n_s, lambda b, s: b * n_s + s) for w in cast]
    outs = pl.pallas_call(
        _with_casts(_mixer_kernel, len(in_specs), len(casts)),
        grid=(bsz, n_s),
        in_specs=in_specs + [c[0] for c in casts],
        out_specs=[tile] + [c[1] for c in casts],
        out_shape=[jax.ShapeDtypeStruct(x.shape, x.dtype)] + [c[2] for c in casts],
        scratch_shapes=[
            pltpu.VMEM((rows, d), _BF16),
            pltpu.VMEM((d // LANES, SUBLANES + rows, LANES), _F32),
            pltpu.VMEM((rows, d), _BF16),
            pltpu.VMEM((rows, d), _F32),
            pltpu.VMEM((rows, d), _F32),
        ],
        compiler_params=pltpu.CompilerParams(
            dimension_semantics=("arbitrary", "arbitrary"),
            vmem_limit_bytes=VMEM_LIMIT_BYTES),
        name="token_mixer",
    )(x, g, w_in, conv_w, ln_g, ln_b, w_s, bias, w_out, *cast)
    return outs[0], outs[1:]


def _ffn_call(x, layer, g, w_up, conv_w, w_down, final_g, *, final_norm, cast, cast_layer):
    bsz, seq, d = x.shape
    d_ff = w_down.shape[0]
    rows = TILE_ROWS
    n_r = rows // SUBLANES
    n_slabs = 2 * d_ff // LANES
    n_tiles = bsz * seq // rows
    x2 = x.reshape(bsz * seq, d)
    stacked = lambda p: _layer(p, layer)
    in_specs = [pl.BlockSpec(memory_space=pl.ANY), stacked(g), _whole(w_up), stacked(conv_w),
                _whole(w_down), _layer(final_g, 0)]
    casts = [_cast_specs(w, cast_layer, n_tiles, lambda t: t) for w in cast]
    body = functools.partial(_ffn_kernel, final_norm=final_norm, col_block=256,
                             tiles_per_seq=seq // rows)
    outs = pl.pallas_call(
        _with_casts(body, len(in_specs), len(casts)),
        grid=(n_tiles,),
        in_specs=in_specs + [c[0] for c in casts],
        out_specs=[pl.BlockSpec(memory_space=pl.ANY)] + [c[1] for c in casts],
        out_shape=[jax.ShapeDtypeStruct(x2.shape, x.dtype)] + [c[2] for c in casts],
        scratch_shapes=[
            pltpu.VMEM((2, n_r, SUBLANES, d), _F32),
            pltpu.VMEM((2, n_r, SUBLANES, d), _F32),
            pltpu.SemaphoreType.DMA((2, SUBLANES)),
            pltpu.SemaphoreType.DMA((2, SUBLANES)),
            pltpu.VMEM((rows, d), _BF16),
            pltpu.VMEM((n_slabs, HIST, LANES), _F32),
            pltpu.VMEM((rows, d_ff), _BF16),
        ],
        compiler_params=pltpu.CompilerParams(
            dimension_semantics=("arbitrary",),
            vmem_limit_bytes=VMEM_LIMIT_BYTES),
        name="channel_mixer",
    )(x2, g, w_up, conv_w, w_down, final_g, *cast)
    return outs[0].reshape(x.shape), outs[1:]


def kernel(x, mix_norm_g, w_in, conv_a_w, ln_v_g, ln_v_b, w_s, b_s, w_out, ffn_norm_g, w_up,
           conv_ffn_w, w_down, final_norm_g):
    depth, d = mix_norm_g.shape
    assert x.shape[1] % TILE_ROWS == 0 and x.shape[1] % MIXER_TILE_ROWS == 0
    assert MIXER_TILE_ROWS % (2 * CHUNK) == 0
    gdim = d // SGU_GROUPS
    rows_of = lambda a: a.reshape(a.shape[0], 1, a.shape[1])
    bias = jnp.repeat(jnp.swapaxes(b_s, 1, 2), gdim, axis=-1)
    final_g = final_norm_g.reshape(1, 1, d)
    wb_in, wb_out = w_in[0].astype(_BF16), w_out[0].astype(_BF16)
    for l in range(depth):
        x, (wb_up, wb_down) = _mixer_call(
            x, l, rows_of(mix_norm_g), wb_in, conv_a_w, rows_of(ln_v_g), rows_of(ln_v_b), w_s, bias,
            wb_out, cast=(w_up, w_down), cast_layer=l)
        last = l == depth - 1
        x, next_weights = _ffn_call(
            x, l, rows_of(ffn_norm_g), wb_up, conv_ffn_w, wb_down, final_g, final_norm=last,
            cast=() if last else (w_in, w_out), cast_layer=l + 1)
        if not last:
            wb_in, wb_out = next_weights
    return x
```

```python
import functools

import jax
import jax.numpy as jnp
from jax import lax
from jax.experimental import pallas as pl
from jax.experimental.pallas import tpu as pltpu

EPS = 1e-6
CHUNK = 128
SGU_GROUPS = 8
CONV_WIDTH = 3
SUBLANES = 8
LANES = 128
BF16_SUBLANES = 16
TILE_ROWS = 512
MIXER_TILE_ROWS = 1024
VMEM_LIMIT_BYTES = 56 * 1024 * 1024

_BF16 = jnp.bfloat16
_F32 = jnp.float32


def _rmsnorm(x, g):
    ms = jnp.mean(x * x, axis=-1, keepdims=True)
    return x * lax.rsqrt(ms + EPS) * g


def _dot(a, b):
    return jnp.dot(a, b, preferred_element_type=_F32)


def _store_slabs(zbuf_ref, first_slab, row0, rows, value):
    for k in range(value.shape[1] // LANES):
        zbuf_ref[first_slab + k, row0:row0 + rows, :] = value[:, k * LANES:(k + 1) * LANES]


def _conv3_offset(zbuf_ref, w_ref, rows, slab):
    w = w_ref[:, slab * LANES:(slab + 1) * LANES]
    z2 = zbuf_ref[slab, SUBLANES - 2:SUBLANES - 2 + rows, :]
    z1 = zbuf_ref[slab, SUBLANES - 1:SUBLANES - 1 + rows, :]
    z0 = zbuf_ref[slab, SUBLANES:SUBLANES + rows, :]
    return z2 * w[0:1, :] + z1 * w[1:2, :] + z0 * w[2:3, :]


def _mixer_kernel(x_ref, g_ref, win_ref, cw_ref, lng_ref, lnb_ref, ws_ref, bias_ref, wout_ref,
                  o_ref, h_ref, zbuf_ref, vn_ref, mixed_ref, merged_ref):
    rows, d = x_ref.shape
    first_tile = pl.program_id(1) == 0
    col = lambda k: slice(k * d, (k + 1) * d)

    h_ref[...] = _rmsnorm(x_ref[...], g_ref[...]).astype(_BF16)
    proj = lambda k: _dot(h_ref[...], win_ref[:, col(k)])

    v = proj(4)
    mu = jnp.mean(v, axis=-1, keepdims=True)
    vc = v - mu
    var = jnp.mean(vc * vc, axis=-1, keepdims=True)
    vn_ref[...] = (vc * lax.rsqrt(var + EPS) * lng_ref[...] + lnb_ref[...]).astype(_BF16)

    prev = zbuf_ref[:, rows:rows + SUBLANES, :]
    zbuf_ref[:, 0:SUBLANES, :] = jnp.where(first_tile, jnp.zeros_like(prev), prev)
    _store_slabs(zbuf_ref, 0, SUBLANES, rows, proj(1) * proj(2))
    conv = jnp.concatenate([_conv3_offset(zbuf_ref, cw_ref, rows, s) for s in range(d // LANES)],
                           axis=1)

    t_idx = lax.broadcasted_iota(jnp.int32, (CHUNK, CHUNK), 0)
    s_idx = lax.broadcasted_iota(jnp.int32, (CHUNK, CHUNK), 1)
    causal = s_idx <= t_idx
    gdim = d // SGU_GROUPS
    n_chunks = rows // CHUNK
    for g in range(SGU_GROUPS):
        wg = jnp.where(causal, ws_ref[g], 0.0).astype(_BF16)
        gcols = slice(g * gdim, (g + 1) * gdim)
        for n in range(0, n_chunks, 2):
            r0 = slice(n * CHUNK, (n + 1) * CHUNK)
            r1 = slice((n + 1) * CHUNK, (n + 2) * CHUNK)
            rhs = jnp.concatenate([vn_ref[r0, gcols], vn_ref[r1, gcols]], axis=1)
            res = _dot(wg, rhs)
            mixed_ref[r0, gcols] = res[:, :gdim]
            mixed_ref[r1, gcols] = res[:, gdim:]
    mixed = mixed_ref[...].reshape(n_chunks, CHUNK, d) + bias_ref[...][None]
    y_b = proj(3) * mixed.reshape(rows, d)
    merged_ref[...] = jax.nn.sigmoid(proj(6)) * y_b

    gated_conv = jax.nn.sigmoid(proj(5)) * conv
    merged = merged_ref[...] + proj(0) * gated_conv

    o_ref[...] = x_ref[...] + _dot(merged.astype(_BF16), wout_ref[...])


HIST = 2 * SUBLANES


def _tile_copies(hbm_ref, buf_ref, sem_ref, tile, slot, to_hbm):
    r = buf_ref.shape[1]
    copies = []
    for sub in range(SUBLANES):
        hbm = hbm_ref.at[pl.ds((tile * SUBLANES + sub) * r, r), :]
        vmem = buf_ref.at[slot, :, sub, :]
        src, dst = (vmem, hbm) if to_hbm else (hbm, vmem)
        copies.append(pltpu.make_async_copy(src, dst, sem_ref.at[slot]))
    return copies


def _conv3_groups(zbuf_ref, w_ref, rows, slab):
    w = w_ref[:, slab * LANES:(slab + 1) * LANES]
    z = zbuf_ref[slab]
    z2, z1, z0 = z[0:rows], z[SUBLANES:SUBLANES + rows], z[HIST:HIST + rows]
    return z2 * w[0:1, :] + z1 * w[1:2, :] + z0 * w[2:3, :]


def _fill_history(zbuf_ref, pbuf_ref, rows, slab):
    last = zbuf_ref[slab, rows:rows + HIST, :]
    prev = pbuf_ref[slab]
    is_last_sublane = lax.broadcasted_iota(jnp.int32, (SUBLANES, LANES), 0) == SUBLANES - 1
    for k in range(2):
        rk = slice(k * SUBLANES, (k + 1) * SUBLANES)
        src = jnp.where(is_last_sublane, prev[rk], last[rk])
        zbuf_ref[slab, rk, :] = pltpu.roll(src, 1, 0)


def _ffn_kernel(x_hbm, g_ref, wup_ref, cw_ref, wdown_ref, fg_ref, o_hbm,
                xbuf, obuf, in_sem, out_sem, h_ref, zbuf_ref, pbuf_ref, act_ref,
                *, final_norm, col_block, tiles_per_seq):
    n_r, _, d = xbuf.shape[1:]
    rows = n_r * SUBLANES
    d_ff = wdown_ref.shape[0]
    slabs_per_block = col_block // LANES
    t = pl.program_id(0)
    n_tiles = pl.num_programs(0)
    slot = lax.rem(t, 2)
    starts_seq = lax.rem(t, tiles_per_seq) == 0

    @pl.when(t == 0)
    def _():
        zbuf_ref[:, rows:rows + HIST, :] = jnp.zeros((zbuf_ref.shape[0], HIST, LANES), _F32)
        for c in _tile_copies(x_hbm, xbuf, in_sem, 0, 0, to_hbm=False):
            c.start()

    @pl.when(t + 1 < n_tiles)
    def _():
        for c in _tile_copies(x_hbm, xbuf, in_sem, t + 1, 1 - slot, to_hbm=False):
            c.start()

    for c in _tile_copies(x_hbm, xbuf, in_sem, t, slot, to_hbm=False):
        c.wait()

    @pl.when(t >= 2)
    def _():
        for c in _tile_copies(o_hbm, obuf, out_sem, t - 2, slot, to_hbm=True):
            c.wait()

    x = xbuf[slot].reshape(rows, d)
    h_ref[...] = _rmsnorm(x, g_ref[...]).astype(_BF16)
    prev = zbuf_ref[:, rows:rows + HIST, :]
    pbuf_ref[...] = jnp.where(starts_seq, jnp.zeros_like(prev), prev)
    for j in range(d_ff // col_block):
        for c0 in (j * col_block, d_ff + j * col_block):
            up = _dot(h_ref[...], wup_ref[:, c0:c0 + col_block])
            _store_slabs(zbuf_ref, c0 // LANES, HIST, rows, up)
            for k in range(slabs_per_block):
                _fill_history(zbuf_ref, pbuf_ref, rows, c0 // LANES + k)
        for k in range(slabs_per_block):
            s = j * slabs_per_block + k
            gate = _conv3_groups(zbuf_ref, cw_ref, rows, s)
            val = _conv3_groups(zbuf_ref, cw_ref, rows, d_ff // LANES + s)
            act_ref[:, s * LANES:(s + 1) * LANES] = (gate * jax.nn.sigmoid(gate) * val).astype(_BF16)
    y = x + _dot(act_ref[...], wdown_ref[...])
    if final_norm:
        y = _rmsnorm(y, fg_ref[...])
    obuf[slot] = y.reshape(n_r, SUBLANES, d)

    for c in _tile_copies(o_hbm, obuf, out_sem, t, slot, to_hbm=True):
        c.start()

    @pl.when(t == n_tiles - 1)
    def _():
        for c in _tile_copies(o_hbm, obuf, out_sem, t, slot, to_hbm=True):
            c.wait()

    @pl.when(jnp.logical_and(t == n_tiles - 1, t >= 1))
    def _():
        for c in _tile_copies(o_hbm, obuf, out_sem, t - 1, 1 - slot, to_hbm=True):
            c.wait()


def _layer(stacked, layer):
    index = (layer,) + (0,) * (stacked.ndim - 1)
    return pl.BlockSpec((None,) + stacked.shape[1:], lambda *_: index, pipeline_mode=pl.Buffered(1))


def _whole(array):
    index = (0,) * array.ndim
    return pl.BlockSpec(array.shape, lambda *_: index, pipeline_mode=pl.Buffered(1))


def _with_casts(body, n_inputs, n_casts):
    def kernel_with_casts(*refs):
        inputs, refs = refs[:n_inputs], refs[n_inputs:]
        cast_src, refs = refs[:n_casts], refs[n_casts:]
        out, refs = refs[0], refs[1:]
        cast_dst, scratch = refs[:n_casts], refs[n_casts:]
        for src, dst in zip(cast_src, cast_dst):
            dst[...] = src[...].astype(_BF16)
        body(*inputs, out, *scratch)
    return kernel_with_casts


def _cast_specs(stacked, layer, n_steps, step_of):
    _, k, n = stacked.shape
    block = next(b for b in range(BF16_SUBLANES, k + 1, BF16_SUBLANES)
                 if k % b == 0 and k // b <= n_steps)
    last = k // block - 1
    src = pl.BlockSpec((None, block, n), lambda *ids: (layer, jnp.minimum(step_of(*ids), last), 0))
    dst = pl.BlockSpec((block, n), lambda *ids: (jnp.minimum(step_of(*ids), last), 0))
    return src, dst, jax.ShapeDtypeStruct((k, n), _BF16)


def _mixer_call(x, layer, g, w_in, conv_w, ln_g, ln_b, w_s, bias, w_out, *, cast, cast_layer):
    bsz, seq, d = x.shape
    rows = MIXER_TILE_ROWS
    n_s = seq // rows
    tile = pl.BlockSpec((None, rows, d), lambda b, s: (b, s, 0))
    stacked = lambda p: _layer(p, layer)
    in_specs = [tile, stacked(g), _whole(w_in), stacked(conv_w), stacked(ln_g), stacked(ln_b),
                stacked(w_s), stacked(bias), _whole(w_out)]
    casts = [_cast_specs(w, cast_layer, bsz * n_s, lambda b, s: b * n_s + s) for w in cast]
    outs = pl.pallas_call(
        _with_casts(_mixer_kernel, len(in_specs), len(casts)),
        grid=(bsz, n_s),
        in_specs=in_specs + [c[0] for c in casts],
        out_specs=[tile] + [c[1] for c in casts],
        out_shape=[jax.ShapeDtypeStruct(x.shape, x.dtype)] + [c[2] for c in casts],
        scratch_shapes=[
            pltpu.VMEM((rows, d), _BF16),
            pltpu.VMEM((d // LANES, SUBLANES + rows, LANES), _F32),
            pltpu.VMEM((rows, d), _BF16),
            pltpu.VMEM((rows, d), _F32),
            pltpu.VMEM((rows, d), _F32),
        ],
        compiler_params=pltpu.CompilerParams(
            dimension_semantics=("arbitrary", "arbitrary"),
            vmem_limit_bytes=VMEM_LIMIT_BYTES),
        name="token_mixer",
    )(x, g, w_in, conv_w, ln_g, ln_b, w_s, bias, w_out, *cast)
    return outs[0], outs[1:]


def _ffn_call(x, layer, g, w_up, conv_w, w_down, final_g, *, final_norm, cast, cast_layer):
    bsz, seq, d = x.shape
    d_ff = w_down.shape[0]
    rows = TILE_ROWS
    n_r = rows // SUBLANES
    n_slabs = 2 * d_ff // LANES
    n_tiles = bsz * seq // rows
    x2 = x.reshape(bsz * seq, d)
    stacked = lambda p: _layer(p, layer)
    in_specs = [pl.BlockSpec(memory_space=pl.ANY), stacked(g), _whole(w_up), stacked(conv_w),
                _whole(w_down), _layer(final_g, 0)]
    casts = [_cast_specs(w, cast_layer, n_tiles, lambda t: t) for w in cast]
    body = functools.partial(_ffn_kernel, final_norm=final_norm, col_block=256,
                             tiles_per_seq=seq // rows)
    outs = pl.pallas_call(
        _with_casts(body, len(in_specs), len(casts)),
        grid=(n_tiles,),
        in_specs=in_specs + [c[0] for c in casts],
        out_specs=[pl.BlockSpec(memory_space=pl.ANY)] + [c[1] for c in casts],
        out_shape=[jax.ShapeDtypeStruct(x2.shape, x.dtype)] + [c[2] for c in casts],
        scratch_shapes=[
            pltpu.VMEM((2, n_r, SUBLANES, d), _F32),
            pltpu.VMEM((2, n_r, SUBLANES, d), _F32),
            pltpu.SemaphoreType.DMA((2,)),
            pltpu.SemaphoreType.DMA((2,)),
            pltpu.VMEM((rows, d), _BF16),
            pltpu.VMEM((n_slabs, HIST + rows, LANES), _F32),
            pltpu.VMEM((n_slabs, HIST, LANES), _F32),
            pltpu.VMEM((rows, d_ff), _BF16),
        ],
        compiler_params=pltpu.CompilerParams(
            dimension_semantics=("arbitrary",),
            vmem_limit_bytes=VMEM_LIMIT_BYTES),
        name="channel_mixer",
    )(x2, g, w_up, conv_w, w_down, final_g, *cast)
    return outs[0].reshape(x.shape), outs[1:]


def kernel(x, mix_norm_g, w_in, conv_a_w, ln_v_g, ln_v_b, w_s, b_s, w_out, ffn_norm_g, w_up,
           conv_ffn_w, w_down, final_norm_g):
    depth, d = mix_norm_g.shape
    assert x.shape[1] % TILE_ROWS == 0 and x.shape[1] % MIXER_TILE_ROWS == 0
    assert MIXER_TILE_ROWS % (2 * CHUNK) == 0
    gdim = d // SGU_GROUPS
    rows_of = lambda a: a.reshape(a.shape[0], 1, a.shape[1])
    bias = jnp.repeat(jnp.swapaxes(b_s, 1, 2), gdim, axis=-1)
    final_g = final_norm_g.reshape(1, 1, d)
    wb_in, wb_out = w_in[0].astype(_BF16), w_out[0].astype(_BF16)
    for l in range(depth):
        x, (wb_up, wb_down) = _mixer_call(
            x, l, rows_of(mix_norm_g), wb_in, conv_a_w, rows_of(ln_v_g), rows_of(ln_v_b), w_s, bias,
            wb_out, cast=(w_up, w_down), cast_layer=l)
        last = l == depth - 1
        x, next_weights = _ffn_call(
            x, l, rows_of(ffn_norm_g), wb_up, conv_ffn_w, wb_down, final_g, final_norm=last,
            cast=() if last else (w_in, w_out), cast_layer=l + 1)
        if not last:
            wb_in, wb_out = next_weights
    return x
```

```python
import functools

import jax
import jax.numpy as jnp
from jax import lax
from jax.experimental import pallas as pl
from jax.experimental.pallas import tpu as pltpu

EPS = 1e-6
CHUNK = 128
SGU_GROUPS = 8
CONV_WIDTH = 3
SUBLANES = 8
LANES = 128
BF16_SUBLANES = 16
TILE_ROWS = 512
MIXER_TILE_ROWS = 1024
HEAD_CHUNK_ROWS = 256
VMEM_LIMIT_BYTES = 56 * 1024 * 1024

_BF16 = jnp.bfloat16
_F32 = jnp.float32


def _rmsnorm(x, g):
    ms = jnp.mean(x * x, axis=-1, keepdims=True)
    return x * lax.rsqrt(ms + EPS) * g


def _dot(a, b):
    return jnp.dot(a, b, preferred_element_type=_F32)


def _store_slabs(zbuf_ref, first_slab, row0, rows, value):
    for k in range(value.shape[1] // LANES):
        zbuf_ref[first_slab + k, row0:row0 + rows, :] = value[:, k * LANES:(k + 1) * LANES]


def _conv3_offset(zbuf_ref, w_ref, rows, slab):
    w = w_ref[:, slab * LANES:(slab + 1) * LANES]
    z2 = zbuf_ref[slab, SUBLANES - 2:SUBLANES - 2 + rows, :]
    z1 = zbuf_ref[slab, SUBLANES - 1:SUBLANES - 1 + rows, :]
    z0 = zbuf_ref[slab, SUBLANES:SUBLANES + rows, :]
    return z2 * w[0:1, :] + z1 * w[1:2, :] + z0 * w[2:3, :]


def _mixer_kernel(x_ref, g_ref, win_ref, cw_ref, lng_ref, lnb_ref, ws_ref, bias_ref, wout_ref,
                  o_ref, h_ref, zbuf_ref, vn_ref, mixed_ref, merged_ref):
    rows, d = x_ref.shape
    first_tile = pl.program_id(1) == 0
    col = lambda k: slice(k * d, (k + 1) * d)

    h_ref[...] = _rmsnorm(x_ref[...], g_ref[...]).astype(_BF16)
    proj = lambda k: _dot(h_ref[...], win_ref[:, col(k)])

    v = jnp.concatenate([_dot(h_ref[r:r + HEAD_CHUNK_ROWS, :], win_ref[:, col(4)])
                         for r in range(0, rows, HEAD_CHUNK_ROWS)], axis=0)
    mu = jnp.mean(v, axis=-1, keepdims=True)
    vc = v - mu
    var = jnp.mean(vc * vc, axis=-1, keepdims=True)
    vn_ref[...] = (vc * lax.rsqrt(var + EPS) * lng_ref[...] + lnb_ref[...]).astype(_BF16)

    prev = zbuf_ref[:, rows:rows + SUBLANES, :]
    zbuf_ref[:, 0:SUBLANES, :] = jnp.where(first_tile, jnp.zeros_like(prev), prev)
    _store_slabs(zbuf_ref, 0, SUBLANES, rows, proj(1) * proj(2))
    conv = jnp.concatenate([_conv3_offset(zbuf_ref, cw_ref, rows, s) for s in range(d // LANES)],
                           axis=1)

    t_idx = lax.broadcasted_iota(jnp.int32, (CHUNK, CHUNK), 0)
    s_idx = lax.broadcasted_iota(jnp.int32, (CHUNK, CHUNK), 1)
    causal = s_idx <= t_idx
    gdim = d // SGU_GROUPS
    n_chunks = rows // CHUNK
    for g in range(SGU_GROUPS):
        wg = jnp.where(causal, ws_ref[g], 0.0).astype(_BF16)
        gcols = slice(g * gdim, (g + 1) * gdim)
        for n in range(0, n_chunks, 2):
            r0 = slice(n * CHUNK, (n + 1) * CHUNK)
            r1 = slice((n + 1) * CHUNK, (n + 2) * CHUNK)
            rhs = jnp.concatenate([vn_ref[r0, gcols], vn_ref[r1, gcols]], axis=1)
            res = _dot(wg, rhs)
            mixed_ref[r0, gcols] = res[:, :gdim]
            mixed_ref[r1, gcols] = res[:, gdim:]
    mixed = mixed_ref[...].reshape(n_chunks, CHUNK, d) + bias_ref[...][None]
    y_b = proj(3) * mixed.reshape(rows, d)
    merged_ref[...] = jax.nn.sigmoid(proj(6)) * y_b

    gated_conv = jax.nn.sigmoid(proj(5)) * conv
    merged = merged_ref[...] + proj(0) * gated_conv

    o_ref[...] = x_ref[...] + _dot(merged.astype(_BF16), wout_ref[...])


HIST = 2 * SUBLANES


def _tile_copies(hbm_ref, buf_ref, sem_ref, tile, slot, to_hbm):
    r = buf_ref.shape[1]
    copies = []
    for sub in range(SUBLANES):
        hbm = hbm_ref.at[pl.ds((tile * SUBLANES + sub) * r, r), :]
        vmem = buf_ref.at[slot, :, sub, :]
        src, dst = (vmem, hbm) if to_hbm else (hbm, vmem)
        copies.append(pltpu.make_async_copy(src, dst, sem_ref.at[slot]))
    return copies


def _conv3_groups(zbuf_ref, w_ref, rows, slab):
    w = w_ref[:, slab * LANES:(slab + 1) * LANES]
    z = zbuf_ref[slab]
    z2, z1, z0 = z[0:rows], z[SUBLANES:SUBLANES + rows], z[HIST:HIST + rows]
    return z2 * w[0:1, :] + z1 * w[1:2, :] + z0 * w[2:3, :]


def _fill_history(zbuf_ref, pbuf_ref, rows, slab):
    last = zbuf_ref[slab, rows:rows + HIST, :]
    prev = pbuf_ref[slab]
    is_last_sublane = lax.broadcasted_iota(jnp.int32, (SUBLANES, LANES), 0) == SUBLANES - 1
    for k in range(2):
        rk = slice(k * SUBLANES, (k + 1) * SUBLANES)
        src = jnp.where(is_last_sublane, prev[rk], last[rk])
        zbuf_ref[slab, rk, :] = pltpu.roll(src, 1, 0)


def _ffn_kernel(x_hbm, g_ref, wup_ref, cw_ref, wdown_ref, fg_ref, o_hbm,
                xbuf, obuf, in_sem, out_sem, h_ref, zbuf_ref, pbuf_ref, act_ref,
                *, final_norm, col_block, tiles_per_seq):
    n_r, _, d = xbuf.shape[1:]
    rows = n_r * SUBLANES
    d_ff = wdown_ref.shape[0]
    slabs_per_block = col_block // LANES
    t = pl.program_id(0)
    n_tiles = pl.num_programs(0)
    slot = lax.rem(t, 2)
    starts_seq = lax.rem(t, tiles_per_seq) == 0

    @pl.when(t == 0)
    def _():
        zbuf_ref[:, rows:rows + HIST, :] = jnp.zeros((zbuf_ref.shape[0], HIST, LANES), _F32)
        for c in _tile_copies(x_hbm, xbuf, in_sem, 0, 0, to_hbm=False):
            c.start()

    @pl.when(t + 1 < n_tiles)
    def _():
        for c in _tile_copies(x_hbm, xbuf, in_sem, t + 1, 1 - slot, to_hbm=False):
            c.start()

    for c in _tile_copies(x_hbm, xbuf, in_sem, t, slot, to_hbm=False):
        c.wait()

    @pl.when(t >= 2)
    def _():
        for c in _tile_copies(o_hbm, obuf, out_sem, t - 2, slot, to_hbm=True):
            c.wait()

    x = xbuf[slot].reshape(rows, d)
    h_ref[...] = _rmsnorm(x, g_ref[...]).astype(_BF16)
    prev = zbuf_ref[:, rows:rows + HIST, :]
    pbuf_ref[...] = jnp.where(starts_seq, jnp.zeros_like(prev), prev)
    for j in range(d_ff // col_block):
        for c0 in (j * col_block, d_ff + j * col_block):
            up = _dot(h_ref[...], wup_ref[:, c0:c0 + col_block])
            _store_slabs(zbuf_ref, c0 // LANES, HIST, rows, up)
            for k in range(slabs_per_block):
                _fill_history(zbuf_ref, pbuf_ref, rows, c0 // LANES + k)
        for k in range(slabs_per_block):
            s = j * slabs_per_block + k
            gate = _conv3_groups(zbuf_ref, cw_ref, rows, s)
            val = _conv3_groups(zbuf_ref, cw_ref, rows, d_ff // LANES + s)
            act_ref[:, s * LANES:(s + 1) * LANES] = (gate * jax.nn.sigmoid(gate) * val).astype(_BF16)
    y = x + _dot(act_ref[...], wdown_ref[...])
    if final_norm:
        y = _rmsnorm(y, fg_ref[...])
    obuf[slot] = y.reshape(n_r, SUBLANES, d)

    for c in _tile_copies(o_hbm, obuf, out_sem, t, slot, to_hbm=True):
        c.start()

    @pl.when(t == n_tiles - 1)
    def _():
        for c in _tile_copies(o_hbm, obuf, out_sem, t, slot, to_hbm=True):
            c.wait()

    @pl.when(jnp.logical_and(t == n_tiles - 1, t >= 1))
    def _():
        for c in _tile_copies(o_hbm, obuf, out_sem, t - 1, 1 - slot, to_hbm=True):
            c.wait()


def _layer(stacked, layer):
    index = (layer,) + (0,) * (stacked.ndim - 1)
    return pl.BlockSpec((None,) + stacked.shape[1:], lambda *_: index, pipeline_mode=pl.Buffered(1))


def _whole(array):
    index = (0,) * array.ndim
    return pl.BlockSpec(array.shape, lambda *_: index, pipeline_mode=pl.Buffered(1))


def _with_casts(body, n_inputs, n_casts):
    def kernel_with_casts(*refs):
        inputs, refs = refs[:n_inputs], refs[n_inputs:]
        cast_src, refs = refs[:n_casts], refs[n_casts:]
        out, refs = refs[0], refs[1:]
        cast_dst, scratch = refs[:n_casts], refs[n_casts:]
        for src, dst in zip(cast_src, cast_dst):
            dst[...] = src[...].astype(_BF16)
        body(*inputs, out, *scratch)
    return kernel_with_casts


def _cast_specs(stacked, layer, n_steps, step_of):
    _, k, n = stacked.shape
    block = next(b for b in range(BF16_SUBLANES, k + 1, BF16_SUBLANES)
                 if k % b == 0 and k // b <= n_steps)
    last = k // block - 1
    src = pl.BlockSpec((None, block, n), lambda *ids: (layer, jnp.minimum(step_of(*ids), last), 0))
    dst = pl.BlockSpec((block, n), lambda *ids: (jnp.minimum(step_of(*ids), last), 0))
    return src, dst, jax.ShapeDtypeStruct((k, n), _BF16)


def _mixer_call(x, layer, g, w_in, conv_w, ln_g, ln_b, w_s, bias, w_out, *, cast, cast_layer):
    bsz, seq, d = x.shape
    rows = MIXER_TILE_ROWS
    n_s = seq // rows
    tile = pl.BlockSpec((None, rows, d), lambda b, s: (b, s, 0))
    stacked = lambda p: _layer(p, layer)
    in_specs = [tile, stacked(g), _whole(w_in), stacked(conv_w), stacked(ln_g), stacked(ln_b),
                stacked(w_s), stacked(bias), _whole(w_out)]
    casts = [_cast_specs(w, cast_layer, bsz * n_s, lambda b, s: b * n_s + s) for w in cast]
    outs = pl.pallas_call(
        _with_casts(_mixer_kernel, len(in_specs), len(casts)),
        grid=(bsz, n_s),
        in_specs=in_specs + [c[0] for c in casts],
        out_specs=[tile] + [c[1] for c in casts],
        out_shape=[jax.ShapeDtypeStruct(x.shape, x.dtype)] + [c[2] for c in casts],
        scratch_shapes=[
            pltpu.VMEM((rows, d), _BF16),
            pltpu.VMEM((d // LANES, SUBLANES + rows, LANES), _F32),
            pltpu.VMEM((rows, d), _BF16),
            pltpu.VMEM((rows, d), _F32),
            pltpu.VMEM((rows, d), _F32),
        ],
        compiler_params=pltpu.CompilerParams(
            dimension_semantics=("arbitrary", "arbitrary"),
            vmem_limit_bytes=VMEM_LIMIT_BYTES),
        name="token_mixer",
    )(x, g, w_in, conv_w, ln_g, ln_b, w_s, bias, w_out, *cast)
    return outs[0], outs[1:]


def _ffn_call(x, layer, g, w_up, conv_w, w_down, final_g, *, final_norm, cast, cast_layer):
    bsz, seq, d = x.shape
    d_ff = w_down.shape[0]
    rows = TILE_ROWS
    n_r = rows // SUBLANES
    n_slabs = 2 * d_ff // LANES
    n_tiles = bsz * seq // rows
    x2 = x.reshape(bsz * seq, d)
    stacked = lambda p: _layer(p, layer)
    in_specs = [pl.BlockSpec(memory_space=pl.ANY), stacked(g), _whole(w_up), stacked(conv_w),
                _whole(w_down), _layer(final_g, 0)]
    casts = [_cast_specs(w, cast_layer, n_tiles, lambda t: t) for w in cast]
    body = functools.partial(_ffn_kernel, final_norm=final_norm, col_block=256,
                             tiles_per_seq=seq // rows)
    outs = pl.pallas_call(
        _with_casts(body, len(in_specs), len(casts)),
        grid=(n_tiles,),
        in_specs=in_specs + [c[0] for c in casts],
        out_specs=[pl.BlockSpec(memory_space=pl.ANY)] + [c[1] for c in casts],
        out_shape=[jax.ShapeDtypeStruct(x2.shape, x.dtype)] + [c[2] for c in casts],
        scratch_shapes=[
            pltpu.VMEM((2, n_r, SUBLANES, d), _F32),
            pltpu.VMEM((2, n_r, SUBLANES, d), _F32),
            pltpu.SemaphoreType.DMA((2,)),
            pltpu.SemaphoreType.DMA((2,)),
            pltpu.VMEM((rows, d), _BF16),
            pltpu.VMEM((n_slabs, HIST + rows, LANES), _F32),
            pltpu.VMEM((n_slabs, HIST, LANES), _F32),
            pltpu.VMEM((rows, d_ff), _BF16),
        ],
        compiler_params=pltpu.CompilerParams(
            dimension_semantics=("arbitrary",),
            vmem_limit_bytes=VMEM_LIMIT_BYTES),
        name="channel_mixer",
    )(x2, g, w_up, conv_w, w_down, final_g, *cast)
    return outs[0].reshape(x.shape), outs[1:]


def kernel(x, mix_norm_g, w_in, conv_a_w, ln_v_g, ln_v_b, w_s, b_s, w_out, ffn_norm_g, w_up,
           conv_ffn_w, w_down, final_norm_g):
    depth, d = mix_norm_g.shape
    assert x.shape[1] % TILE_ROWS == 0 and x.shape[1] % MIXER_TILE_ROWS == 0
    assert MIXER_TILE_ROWS % (2 * CHUNK) == 0
    gdim = d // SGU_GROUPS
    rows_of = lambda a: a.reshape(a.shape[0], 1, a.shape[1])
    bias = jnp.repeat(jnp.swapaxes(b_s, 1, 2), gdim, axis=-1)
    final_g = final_norm_g.reshape(1, 1, d)
    wb_in, wb_out = w_in[0].astype(_BF16), w_out[0].astype(_BF16)
    for l in range(depth):
        x, (wb_up, wb_down) = _mixer_call(
            x, l, rows_of(mix_norm_g), wb_in, conv_a_w, rows_of(ln_v_g), rows_of(ln_v_b), w_s, bias,
            wb_out, cast=(w_up, w_down), cast_layer=l)
        last = l == depth - 1
        x, next_weights = _ffn_call(
            x, l, rows_of(ffn_norm_g), wb_up, conv_ffn_w, wb_down, final_g, final_norm=last,
            cast=() if last else (w_in, w_out), cast_layer=l + 1)
        if not last:
            wb_in, wb_out = next_weights
    return x
```

```python
import functools

import jax
import jax.numpy as jnp
from jax import lax
from jax.experimental import pallas as pl
from jax.experimental.pallas import tpu as pltpu

EPS = 1e-6
CHUNK = 128
SGU_GROUPS = 8
CONV_WIDTH = 3
SUBLANES = 8
LANES = 128
BF16_SUBLANES = 16
TILE_ROWS = 512
MIXER_TILE_ROWS = 1024
HEAD_CHUNK_ROWS = 256
VMEM_LIMIT_BYTES = 56 * 1024 * 1024

_BF16 = jnp.bfloat16
_F32 = jnp.float32


def _rmsnorm(x, g):
    ms = jnp.mean(x * x, axis=-1, keepdims=True)
    return x * lax.rsqrt(ms + EPS) * g


def _dot(a, b):
    return jnp.dot(a, b, preferred_element_type=_F32)


def _store_slabs(zbuf_ref, first_slab, row0, rows, value):
    for k in range(value.shape[1] // LANES):
        zbuf_ref[first_slab + k, row0:row0 + rows, :] = value[:, k * LANES:(k + 1) * LANES]


def _conv3_offset(zbuf_ref, w_ref, rows, slab):
    w = w_ref[:, slab * LANES:(slab + 1) * LANES]
    z2 = zbuf_ref[slab, SUBLANES - 2:SUBLANES - 2 + rows, :]
    z1 = zbuf_ref[slab, SUBLANES - 1:SUBLANES - 1 + rows, :]
    z0 = zbuf_ref[slab, SUBLANES:SUBLANES + rows, :]
    return z2 * w[0:1, :] + z1 * w[1:2, :] + z0 * w[2:3, :]


def _mixer_kernel(x_ref, g_ref, win_ref, cw_ref, lng_ref, lnb_ref, ws_ref, bias_ref, wout_ref,
                  o_ref, h_ref, zbuf_ref, vn_ref, mixed_ref, merged_ref):
    rows, d = x_ref.shape
    first_tile = pl.program_id(1) == 0
    col = lambda k: slice(k * d, (k + 1) * d)

    h_ref[...] = _rmsnorm(x_ref[...], g_ref[...]).astype(_BF16)
    proj = lambda k: _dot(h_ref[...], win_ref[:, col(k)])

    v = jnp.concatenate([_dot(h_ref[r:r + HEAD_CHUNK_ROWS, :], win_ref[:, col(4)])
                         for r in range(0, rows, HEAD_CHUNK_ROWS)], axis=0)
    mu = jnp.mean(v, axis=-1, keepdims=True)
    vc = v - mu
    var = jnp.mean(vc * vc, axis=-1, keepdims=True)
    vn_ref[...] = (vc * lax.rsqrt(var + EPS) * lng_ref[...] + lnb_ref[...]).astype(_BF16)

    prev = zbuf_ref[:, rows:rows + SUBLANES, :]
    zbuf_ref[:, 0:SUBLANES, :] = jnp.where(first_tile, jnp.zeros_like(prev), prev)
    _store_slabs(zbuf_ref, 0, SUBLANES, rows, proj(1) * proj(2))
    conv = jnp.concatenate([_conv3_offset(zbuf_ref, cw_ref, rows, s) for s in range(d // LANES)],
                           axis=1)

    t_idx = lax.broadcasted_iota(jnp.int32, (CHUNK, CHUNK), 0)
    s_idx = lax.broadcasted_iota(jnp.int32, (CHUNK, CHUNK), 1)
    causal = s_idx <= t_idx
    gdim = d // SGU_GROUPS
    n_chunks = rows // CHUNK
    for g in range(SGU_GROUPS):
        wg = jnp.where(causal, ws_ref[g], 0.0).astype(_BF16)
        gcols = slice(g * gdim, (g + 1) * gdim)
        for n in range(0, n_chunks, 2):
            r0 = slice(n * CHUNK, (n + 1) * CHUNK)
            r1 = slice((n + 1) * CHUNK, (n + 2) * CHUNK)
            rhs = jnp.concatenate([vn_ref[r0, gcols], vn_ref[r1, gcols]], axis=1)
            res = _dot(wg, rhs)
            mixed_ref[r0, gcols] = res[:, :gdim]
            mixed_ref[r1, gcols] = res[:, gdim:]
    mixed = mixed_ref[...].reshape(n_chunks, CHUNK, d) + bias_ref[...][None]
    y_b = proj(3) * mixed.reshape(rows, d)
    merged_ref[...] = jax.nn.sigmoid(proj(6)) * y_b

    gated_conv = jax.nn.sigmoid(proj(5)) * conv
    merged = merged_ref[...] + proj(0) * gated_conv

    o_ref[...] = x_ref[...] + _dot(merged.astype(_BF16), wout_ref[...])


HIST = 2 * SUBLANES
SLAB_PAD_ROWS = 8


def _tile_copies(hbm_ref, buf_ref, sem_ref, tile, slot, to_hbm):
    r = buf_ref.shape[1]
    copies = []
    for sub in range(SUBLANES):
        hbm = hbm_ref.at[pl.ds((tile * SUBLANES + sub) * r, r), :]
        vmem = buf_ref.at[slot, :, sub, :]
        src, dst = (vmem, hbm) if to_hbm else (hbm, vmem)
        copies.append(pltpu.make_async_copy(src, dst, sem_ref.at[slot]))
    return copies


def _conv3_groups(zbuf_ref, w_ref, rows, slab):
    w = w_ref[:, slab * LANES:(slab + 1) * LANES]
    z = zbuf_ref[slab, 0:HIST + rows, :]
    z2, z1, z0 = z[0:rows], z[SUBLANES:SUBLANES + rows], z[HIST:HIST + rows]
    return z2 * w[0:1, :] + z1 * w[1:2, :] + z0 * w[2:3, :]


def _fill_history(zbuf_ref, pbuf_ref, rows, slab):
    last = zbuf_ref[slab, rows:rows + HIST, :]
    prev = pbuf_ref[slab]
    is_last_sublane = lax.broadcasted_iota(jnp.int32, (SUBLANES, LANES), 0) == SUBLANES - 1
    for k in range(2):
        rk = slice(k * SUBLANES, (k + 1) * SUBLANES)
        src = jnp.where(is_last_sublane, prev[rk], last[rk])
        zbuf_ref[slab, rk, :] = pltpu.roll(src, 1, 0)


def _ffn_kernel(x_hbm, g_ref, wup_ref, cw_ref, wdown_ref, fg_ref, o_hbm,
                xbuf, obuf, in_sem, out_sem, h_ref, zbuf_ref, pbuf_ref, act_ref,
                *, final_norm, col_block, tiles_per_seq):
    n_r, _, d = xbuf.shape[1:]
    rows = n_r * SUBLANES
    d_ff = wdown_ref.shape[0]
    slabs_per_block = col_block // LANES
    t = pl.program_id(0)
    n_tiles = pl.num_programs(0)
    slot = lax.rem(t, 2)
    starts_seq = lax.rem(t, tiles_per_seq) == 0

    @pl.when(t == 0)
    def _():
        zbuf_ref[:, rows:rows + HIST, :] = jnp.zeros((zbuf_ref.shape[0], HIST, LANES), _F32)
        for c in _tile_copies(x_hbm, xbuf, in_sem, 0, 0, to_hbm=False):
            c.start()

    @pl.when(t + 1 < n_tiles)
    def _():
        for c in _tile_copies(x_hbm, xbuf, in_sem, t + 1, 1 - slot, to_hbm=False):
            c.start()

    for c in _tile_copies(x_hbm, xbuf, in_sem, t, slot, to_hbm=False):
        c.wait()

    @pl.when(t >= 2)
    def _():
        for c in _tile_copies(o_hbm, obuf, out_sem, t - 2, slot, to_hbm=True):
            c.wait()

    x = xbuf[slot].reshape(rows, d)
    h_ref[...] = _rmsnorm(x, g_ref[...]).astype(_BF16)
    prev = zbuf_ref[:, rows:rows + HIST, :]
    pbuf_ref[...] = jnp.where(starts_seq, jnp.zeros_like(prev), prev)
    for j in range(d_ff // col_block):
        for c0 in (j * col_block, d_ff + j * col_block):
            up = _dot(h_ref[...], wup_ref[:, c0:c0 + col_block])
            _store_slabs(zbuf_ref, c0 // LANES, HIST, rows, up)
            for k in range(slabs_per_block):
                _fill_history(zbuf_ref, pbuf_ref, rows, c0 // LANES + k)
        for k in range(slabs_per_block):
            s = j * slabs_per_block + k
            gate = _conv3_groups(zbuf_ref, cw_ref, rows, s)
            val = _conv3_groups(zbuf_ref, cw_ref, rows, d_ff // LANES + s)
            act_ref[:, s * LANES:(s + 1) * LANES] = (gate * jax.nn.sigmoid(gate) * val).astype(_BF16)
    y = x + _dot(act_ref[...], wdown_ref[...])
    if final_norm:
        y = _rmsnorm(y, fg_ref[...])
    obuf[slot] = y.reshape(n_r, SUBLANES, d)

    for c in _tile_copies(o_hbm, obuf, out_sem, t, slot, to_hbm=True):
        c.start()

    @pl.when(t == n_tiles - 1)
    def _():
        for c in _tile_copies(o_hbm, obuf, out_sem, t, slot, to_hbm=True):
            c.wait()

    @pl.when(jnp.logical_and(t == n_tiles - 1, t >= 1))
    def _():
        for c in _tile_copies(o_hbm, obuf, out_sem, t - 1, 1 - slot, to_hbm=True):
            c.wait()


def _layer(stacked, layer):
    index = (layer,) + (0,) * (stacked.ndim - 1)
    return pl.BlockSpec((None,) + stacked.shape[1:], lambda *_: index, pipeline_mode=pl.Buffered(1))


def _whole(array):
    index = (0,) * array.ndim
    return pl.BlockSpec(array.shape, lambda *_: index, pipeline_mode=pl.Buffered(1))


def _with_casts(body, n_inputs, n_casts):
    def kernel_with_casts(*refs):
        inputs, refs = refs[:n_inputs], refs[n_inputs:]
        cast_src, refs = refs[:n_casts], refs[n_casts:]
        out, refs = refs[0], refs[1:]
        cast_dst, scratch = refs[:n_casts], refs[n_casts:]
        for src, dst in zip(cast_src, cast_dst):
            dst[...] = src[...].astype(_BF16)
        body(*inputs, out, *scratch)
    return kernel_with_casts


def _cast_specs(stacked, layer, n_steps, step_of):
    _, k, n = stacked.shape
    block = next(b for b in range(BF16_SUBLANES, k + 1, BF16_SUBLANES)
                 if k % b == 0 and k // b <= n_steps)
    last = k // block - 1
    src = pl.BlockSpec((None, block, n), lambda *ids: (layer, jnp.minimum(step_of(*ids), last), 0))
    dst = pl.BlockSpec((block, n), lambda *ids: (jnp.minimum(step_of(*ids), last), 0))
    return src, dst, jax.ShapeDtypeStruct((k, n), _BF16)


def _mixer_call(x, layer, g, w_in, conv_w, ln_g, ln_b, w_s, bias, w_out, *, cast, cast_layer):
    bsz, seq, d = x.shape
    rows = MIXER_TILE_ROWS
    n_s = seq // rows
    tile = pl.BlockSpec((None, rows, d), lambda b, s: (b, s, 0))
    stacked = lambda p: _layer(p, layer)
    in_specs = [tile, stacked(g), _whole(w_in), stacked(conv_w), stacked(ln_g), stacked(ln_b),
                stacked(w_s), stacked(bias), _whole(w_out)]
    casts = [_cast_specs(w, cast_layer, bsz * n_s, lambda b, s: b * n_s + s) for w in cast]
    outs = pl.pallas_call(
        _with_casts(_mixer_kernel, len(in_specs), len(casts)),
        grid=(bsz, n_s),
        in_specs=in_specs + [c[0] for c in casts],
        out_specs=[tile] + [c[1] for c in casts],
        out_shape=[jax.ShapeDtypeStruct(x.shape, x.dtype)] + [c[2] for c in casts],
        scratch_shapes=[
            pltpu.VMEM((rows, d), _BF16),
            pltpu.VMEM((d // LANES, SUBLANES + rows, LANES), _F32),
            pltpu.VMEM((rows, d), _BF16),
            pltpu.VMEM((rows, d), _F32),
            pltpu.VMEM((rows, d), _F32),
        ],
        compiler_params=pltpu.CompilerParams(
            dimension_semantics=("arbitrary", "arbitrary"),
            vmem_limit_bytes=VMEM_LIMIT_BYTES),
        name="token_mixer",
    )(x, g, w_in, conv_w, ln_g, ln_b, w_s, bias, w_out, *cast)
    return outs[0], outs[1:]


def _ffn_call(x, layer, g, w_up, conv_w, w_down, final_g, *, final_norm, cast, cast_layer):
    bsz, seq, d = x.shape
    d_ff = w_down.shape[0]
    rows = TILE_ROWS
    n_r = rows // SUBLANES
    n_slabs = 2 * d_ff // LANES
    n_tiles = bsz * seq // rows
    x2 = x.reshape(bsz * seq, d)
    stacked = lambda p: _layer(p, layer)
    in_specs = [pl.BlockSpec(memory_space=pl.ANY), stacked(g), _whole(w_up), stacked(conv_w),
                _whole(w_down), _layer(final_g, 0)]
    casts = [_cast_specs(w, cast_layer, n_tiles, lambda t: t) for w in cast]
    body = functools.partial(_ffn_kernel, final_norm=final_norm, col_block=256,
                             tiles_per_seq=seq // rows)
    outs = pl.pallas_call(
        _with_casts(body, len(in_specs), len(casts)),
        grid=(n_tiles,),
        in_specs=in_specs + [c[0] for c in casts],
        out_specs=[pl.BlockSpec(memory_space=pl.ANY)] + [c[1] for c in casts],
        out_shape=[jax.ShapeDtypeStruct(x2.shape, x.dtype)] + [c[2] for c in casts],
        scratch_shapes=[
            pltpu.VMEM((2, n_r, SUBLANES, d), _F32),
            pltpu.VMEM((2, n_r, SUBLANES, d), _F32),
            pltpu.SemaphoreType.DMA((2,)),
            pltpu.SemaphoreType.DMA((2,)),
            pltpu.VMEM((rows, d), _BF16),
            pltpu.VMEM((n_slabs, HIST + rows + SLAB_PAD_ROWS, LANES), _F32),
            pltpu.VMEM((n_slabs, HIST, LANES), _F32),
            pltpu.VMEM((rows, d_ff), _BF16),
        ],
        compiler_params=pltpu.CompilerParams(
            dimension_semantics=("arbitrary",),
            vmem_limit_bytes=VMEM_LIMIT_BYTES),
        name="channel_mixer",
    )(x2, g, w_up, conv_w, w_down, final_g, *cast)
    return outs[0].reshape(x.shape), outs[1:]


def kernel(x, mix_norm_g, w_in, conv_a_w, ln_v_g, ln_v_b, w_s, b_s, w_out, ffn_norm_g, w_up,
           conv_ffn_w, w_down, final_norm_g):
    depth, d = mix_norm_g.shape
    assert x.shape[1] % TILE_ROWS == 0 and x.shape[1] % MIXER_TILE_ROWS == 0
    assert MIXER_TILE_ROWS % (2 * CHUNK) == 0
    gdim = d // SGU_GROUPS
    rows_of = lambda a: a.reshape(a.shape[0], 1, a.shape[1])
    bias = jnp.repeat(jnp.swapaxes(b_s, 1, 2), gdim, axis=-1)
    final_g = final_norm_g.reshape(1, 1, d)
    wb_in, wb_out = w_in[0].astype(_BF16), w_out[0].astype(_BF16)
    for l in range(depth):
        x, (wb_up, wb_down) = _mixer_call(
            x, l, rows_of(mix_norm_g), wb_in, conv_a_w, rows_of(ln_v_g), rows_of(ln_v_b), w_s, bias,
            wb_out, cast=(w_up, w_down), cast_layer=l)
        last = l == depth - 1
        x, next_weights = _ffn_call(
            x, l, rows_of(ffn_norm_g), wb_up, conv_ffn_w, wb_down, final_g, final_norm=last,
            cast=() if last else (w_in, w_out), cast_layer=l + 1)
        if not last:
            wb_in, wb_out = next_weights
    return x
```

```python
import functools

import jax
import jax.numpy as jnp
from jax import lax
from jax.experimental import pallas as pl
from jax.experimental.pallas import tpu as pltpu

EPS = 1e-6
CHUNK = 128
SGU_GROUPS = 8
CONV_WIDTH = 3
SUBLANES = 8
LANES = 128
BF16_SUBLANES = 16
TILE_ROWS = 512
MIXER_TILE_ROWS = 1024
HEAD_CHUNK_ROWS = 256
VMEM_LIMIT_BYTES = 56 * 1024 * 1024

B_GATE, C_GATE, XIN, U, V, G_A, G_B = range(7)

_BF16 = jnp.bfloat16
_F32 = jnp.float32


def _rmsnorm(x, g):
    ms = jnp.mean(x * x, axis=-1, keepdims=True)
    return x * lax.rsqrt(ms + EPS) * g


def _dot(a, b):
    return jnp.dot(a, b, preferred_element_type=_F32)


def _store_slabs(zbuf_ref, first_slab, row0, rows, value):
    for k in range(value.shape[1] // LANES):
        zbuf_ref[first_slab + k, row0:row0 + rows, :] = value[:, k * LANES:(k + 1) * LANES]


def _conv3_offset(zbuf_ref, w_ref, rows, slab):
    w = w_ref[:, slab * LANES:(slab + 1) * LANES]
    z2 = zbuf_ref[slab, SUBLANES - 2:SUBLANES - 2 + rows, :]
    z1 = zbuf_ref[slab, SUBLANES - 1:SUBLANES - 1 + rows, :]
    z0 = zbuf_ref[slab, SUBLANES:SUBLANES + rows, :]
    return z2 * w[0:1, :] + z1 * w[1:2, :] + z0 * w[2:3, :]


def _mixer_kernel(x_ref, g_ref, win_ref, cw_ref, lng_ref, lnb_ref, ws_ref, bias_ref, wout_ref,
                  o_ref, h_ref, zbuf_ref, vn_ref, mixed_ref, merged_ref):
    rows, d = x_ref.shape
    first_tile = pl.program_id(1) == 0
    col = lambda k: slice(k * d, (k + 1) * d)

    h_ref[...] = _rmsnorm(x_ref[...], g_ref[...]).astype(_BF16)
    proj = lambda k: _dot(h_ref[...], win_ref[:, col(k)])

    v = jnp.concatenate([_dot(h_ref[r:r + HEAD_CHUNK_ROWS, :], win_ref[:, col(V)])
                         for r in range(0, rows, HEAD_CHUNK_ROWS)], axis=0)
    mu = jnp.mean(v, axis=-1, keepdims=True)
    vc = v - mu
    var = jnp.mean(vc * vc, axis=-1, keepdims=True)
    vn_ref[...] = (vc * lax.rsqrt(var + EPS) * lng_ref[...] + lnb_ref[...]).astype(_BF16)

    prev = zbuf_ref[:, rows:rows + SUBLANES, :]
    zbuf_ref[:, 0:SUBLANES, :] = jnp.where(first_tile, jnp.zeros_like(prev), prev)
    _store_slabs(zbuf_ref, 0, SUBLANES, rows, proj(C_GATE) * proj(XIN))
    conv = jnp.concatenate([_conv3_offset(zbuf_ref, cw_ref, rows, s) for s in range(d // LANES)],
                           axis=1)

    t_idx = lax.broadcasted_iota(jnp.int32, (CHUNK, CHUNK), 0)
    s_idx = lax.broadcasted_iota(jnp.int32, (CHUNK, CHUNK), 1)
    causal = s_idx <= t_idx
    gdim = d // SGU_GROUPS
    n_chunks = rows // CHUNK
    for g in range(SGU_GROUPS):
        wg = jnp.where(causal, ws_ref[g], 0.0).astype(_BF16)
        gcols = slice(g * gdim, (g + 1) * gdim)
        for n in range(0, n_chunks, 2):
            r0 = slice(n * CHUNK, (n + 1) * CHUNK)
            r1 = slice((n + 1) * CHUNK, (n + 2) * CHUNK)
            rhs = jnp.concatenate([vn_ref[r0, gcols], vn_ref[r1, gcols]], axis=1)
            res = _dot(wg, rhs)
            mixed_ref[r0, gcols] = res[:, :gdim]
            mixed_ref[r1, gcols] = res[:, gdim:]
    mixed = mixed_ref[...].reshape(n_chunks, CHUNK, d) + bias_ref[...][None]
    y_b = proj(U) * mixed.reshape(rows, d)
    merged_ref[...] = jax.nn.sigmoid(proj(G_B)) * y_b

    gated_conv = jax.nn.sigmoid(proj(G_A)) * conv
    merged = merged_ref[...] + proj(B_GATE) * gated_conv

    o_ref[...] = x_ref[...] + _dot(merged.astype(_BF16), wout_ref[...])


HIST = 2 * SUBLANES


def _tile_copies(hbm_ref, buf_ref, sem_ref, tile, slot, to_hbm):
    r = buf_ref.shape[1]
    copies = []
    for sub in range(SUBLANES):
        hbm = hbm_ref.at[pl.ds((tile * SUBLANES + sub) * r, r), :]
        vmem = buf_ref.at[slot, :, sub, :]
        src, dst = (vmem, hbm) if to_hbm else (hbm, vmem)
        copies.append(pltpu.make_async_copy(src, dst, sem_ref.at[slot]))
    return copies


def _conv3_groups(zbuf_ref, w_ref, rows, slab):
    w = w_ref[:, slab * LANES:(slab + 1) * LANES]
    z = zbuf_ref[slab]
    z2, z1, z0 = z[0:rows], z[SUBLANES:SUBLANES + rows], z[HIST:HIST + rows]
    return z2 * w[0:1, :] + z1 * w[1:2, :] + z0 * w[2:3, :]


def _fill_history(zbuf_ref, prev, rows, slab):
    last = zbuf_ref[slab, rows:rows + HIST, :]
    is_last_sublane = lax.broadcasted_iota(jnp.int32, (SUBLANES, LANES), 0) == SUBLANES - 1
    for k in range(2):
        rk = slice(k * SUBLANES, (k + 1) * SUBLANES)
        src = jnp.where(is_last_sublane, prev[rk], last[rk])
        zbuf_ref[slab, rk, :] = pltpu.roll(src, 1, 0)


def _ffn_kernel(x_hbm, g_ref, wup_ref, cw_ref, wdown_ref, fg_ref, o_hbm,
                xbuf, obuf, in_sem, out_sem, h_ref, zbuf_ref, act_ref,
                *, final_norm, col_block, tiles_per_seq):
    n_r, _, d = xbuf.shape[1:]
    rows = n_r * SUBLANES
    d_ff = wdown_ref.shape[0]
    slabs_per_block = col_block // LANES
    t = pl.program_id(0)
    n_tiles = pl.num_programs(0)
    slot = lax.rem(t, 2)
    starts_seq = lax.rem(t, tiles_per_seq) == 0

    @pl.when(t == 0)
    def _():
        zbuf_ref[:, rows:rows + HIST, :] = jnp.zeros((zbuf_ref.shape[0], HIST, LANES), _F32)
        for c in _tile_copies(x_hbm, xbuf, in_sem, 0, 0, to_hbm=False):
            c.start()

    @pl.when(t + 1 < n_tiles)
    def _():
        for c in _tile_copies(x_hbm, xbuf, in_sem, t + 1, 1 - slot, to_hbm=False):
            c.start()

    for c in _tile_copies(x_hbm, xbuf, in_sem, t, slot, to_hbm=False):
        c.wait()

    @pl.when(t >= 2)
    def _():
        for c in _tile_copies(o_hbm, obuf, out_sem, t - 2, slot, to_hbm=True):
            c.wait()

    x = xbuf[slot].reshape(rows, d)
    h_ref[...] = _rmsnorm(x, g_ref[...]).astype(_BF16)
    for j in range(d_ff // col_block):
        for c0 in (j * col_block, d_ff + j * col_block):
            up = _dot(h_ref[...], wup_ref[:, c0:c0 + col_block])
            slabs = range(c0 // LANES, c0 // LANES + slabs_per_block)
            prev = [zbuf_ref[s, rows:rows + HIST, :] for s in slabs]
            _store_slabs(zbuf_ref, c0 // LANES, HIST, rows, up)
            for s, p in zip(slabs, prev):
                _fill_history(zbuf_ref, jnp.where(starts_seq, jnp.zeros_like(p), p), rows, s)
        for k in range(slabs_per_block):
            s = j * slabs_per_block + k
            gate = _conv3_groups(zbuf_ref, cw_ref, rows, s)
            val = _conv3_groups(zbuf_ref, cw_ref, rows, d_ff // LANES + s)
            act_ref[:, s * LANES:(s + 1) * LANES] = (gate * jax.nn.sigmoid(gate) * val).astype(_BF16)
    y = x + _dot(act_ref[...], wdown_ref[...])
    if final_norm:
        y = _rmsnorm(y, fg_ref[...])
    obuf[slot] = y.reshape(n_r, SUBLANES, d)

    for c in _tile_copies(o_hbm, obuf, out_sem, t, slot, to_hbm=True):
        c.start()

    @pl.when(t == n_tiles - 1)
    def _():
        for c in _tile_copies(o_hbm, obuf, out_sem, t, slot, to_hbm=True):
            c.wait()

    @pl.when(jnp.logical_and(t == n_tiles - 1, t >= 1))
    def _():
        for c in _tile_copies(o_hbm, obuf, out_sem, t - 1, 1 - slot, to_hbm=True):
            c.wait()


def _layer(stacked, layer):
    index = (layer,) + (0,) * (stacked.ndim - 1)
    return pl.BlockSpec((None,) + stacked.shape[1:], lambda *_: index, pipeline_mode=pl.Buffered(1))


def _whole(array):
    index = (0,) * array.ndim
    return pl.BlockSpec(array.shape, lambda *_: index, pipeline_mode=pl.Buffered(1))


def _with_casts(body, n_inputs, n_casts):
    def kernel_with_casts(*refs):
        inputs, refs = refs[:n_inputs], refs[n_inputs:]
        cast_src, refs = refs[:n_casts], refs[n_casts:]
        out, refs = refs[0], refs[1:]
        cast_dst, scratch = refs[:n_casts], refs[n_casts:]
        for src, dst in zip(cast_src, cast_dst):
            dst[...] = src[...].astype(_BF16)
        body(*inputs, out, *scratch)
    return kernel_with_casts


def _cast_specs(stacked, layer, n_steps, step_of):
    _, k, n = stacked.shape
    block = next(b for b in range(BF16_SUBLANES, k + 1, BF16_SUBLANES)
                 if k % b == 0 and k // b <= n_steps)
    last = k // block - 1
    src = pl.BlockSpec((None, block, n), lambda *ids: (layer, jnp.minimum(step_of(*ids), last), 0))
    dst = pl.BlockSpec((block, n), lambda *ids: (jnp.minimum(step_of(*ids), last), 0))
    return src, dst, jax.ShapeDtypeStruct((k, n), _BF16)


def _mixer_call(x, layer, g, w_in, conv_w, ln_g, ln_b, w_s, bias, w_out, *, cast, cast_layer):
    bsz, seq, d = x.shape
    rows = MIXER_TILE_ROWS
    n_s = seq // rows
    tile = pl.BlockSpec((None, rows, d), lambda b, s: (b, s, 0))
    stacked = lambda p: _layer(p, layer)
    in_specs = [tile, stacked(g), _whole(w_in), stacked(conv_w), stacked(ln_g), stacked(ln_b),
                stacked(w_s), stacked(bias), _whole(w_out)]
    casts = [_cast_specs(w, cast_layer, bsz * n_s, lambda b, s: b * n_s + s) for w in cast]
    outs = pl.pallas_call(
        _with_casts(_mixer_kernel, len(in_specs), len(casts)),
        grid=(bsz, n_s),
        in_specs=in_specs + [c[0] for c in casts],
        out_specs=[tile] + [c[1] for c in casts],
        out_shape=[jax.ShapeDtypeStruct(x.shape, x.dtype)] + [c[2] for c in casts],
        scratch_shapes=[
            pltpu.VMEM((rows, d), _BF16),
            pltpu.VMEM((d // LANES, SUBLANES + rows, LANES), _F32),
            pltpu.VMEM((rows, d), _BF16),
            pltpu.VMEM((rows, d), _F32),
            pltpu.VMEM((rows, d), _F32),
        ],
        compiler_params=pltpu.CompilerParams(
            dimension_semantics=("arbitrary", "arbitrary"),
            vmem_limit_bytes=VMEM_LIMIT_BYTES),
        name="token_mixer",
    )(x, g, w_in, conv_w, ln_g, ln_b, w_s, bias, w_out, *cast)
    return outs[0], outs[1:]


def _ffn_call(x, layer, g, w_up, conv_w, w_down, final_g, *, final_norm, cast, cast_layer):
    bsz, seq, d = x.shape
    d_ff = w_down.shape[0]
    rows = TILE_ROWS
    n_r = rows // SUBLANES
    n_slabs = 2 * d_ff // LANES
    n_tiles = bsz * seq // rows
    x2 = x.reshape(bsz * seq, d)
    stacked = lambda p: _layer(p, layer)
    in_specs = [pl.BlockSpec(memory_space=pl.ANY), stacked(g), _whole(w_up), stacked(conv_w),
                _whole(w_down), _layer(final_g, 0)]
    casts = [_cast_specs(w, cast_layer, n_tiles, lambda t: t) for w in cast]
    body = functools.partial(_ffn_kernel, final_norm=final_norm, col_block=256,
                             tiles_per_seq=seq // rows)
    outs = pl.pallas_call(
        _with_casts(body, len(in_specs), len(casts)),
        grid=(n_tiles,),
        in_specs=in_specs + [c[0] for c in casts],
        out_specs=[pl.BlockSpec(memory_space=pl.ANY)] + [c[1] for c in casts],
        out_shape=[jax.ShapeDtypeStruct(x2.shape, x.dtype)] + [c[2] for c in casts],
        scratch_shapes=[
            pltpu.VMEM((2, n_r, SUBLANES, d), _F32),
            pltpu.VMEM((2, n_r, SUBLANES, d), _F32),
            pltpu.SemaphoreType.DMA((2,)),
            pltpu.SemaphoreType.DMA((2,)),
            pltpu.VMEM((rows, d), _BF16),
            pltpu.VMEM((n_slabs, HIST + rows, LANES), _F32),
            pltpu.VMEM((rows, d_ff), _BF16),
        ],
        compiler_params=pltpu.CompilerParams(
            dimension_semantics=("arbitrary",),
            vmem_limit_bytes=VMEM_LIMIT_BYTES),
        name="channel_mixer",
    )(x2, g, w_up, conv_w, w_down, final_g, *cast)
    return outs[0].reshape(x.shape), outs[1:]


def kernel(x, mix_norm_g, w_in, conv_a_w, ln_v_g, ln_v_b, w_s, b_s, w_out, ffn_norm_g, w_up,
           conv_ffn_w, w_down, final_norm_g):
    depth, d = mix_norm_g.shape
    assert x.shape[1] % TILE_ROWS == 0 and x.shape[1] % MIXER_TILE_ROWS == 0
    assert MIXER_TILE_ROWS % (2 * CHUNK) == 0
    gdim = d // SGU_GROUPS
    rows_of = lambda a: a.reshape(a.shape[0], 1, a.shape[1])
    bias = jnp.repeat(jnp.swapaxes(b_s, 1, 2), gdim, axis=-1)
    final_g = final_norm_g.reshape(1, 1, d)
    wb_in, wb_out = w_in[0].astype(_BF16), w_out[0].astype(_BF16)
    for l in range(depth):
        x, (wb_up, wb_down) = _mixer_call(
            x, l, rows_of(mix_norm_g), wb_in, conv_a_w, rows_of(ln_v_g), rows_of(ln_v_b), w_s, bias,
            wb_out, cast=(w_up, w_down), cast_layer=l)
        last = l == depth - 1
        x, next_weights = _ffn_call(
            x, l, rows_of(ffn_norm_g), wb_up, conv_ffn_w, wb_down, final_g, final_norm=last,
            cast=() if last else (w_in, w_out), cast_layer=l + 1)
        if not last:
            wb_in, wb_out = next_weights
    return x
```

```python
import functools

import jax
import jax.numpy as jnp
from jax import lax
from jax.experimental import pallas as pl
from jax.experimental.pallas import tpu as pltpu

EPS = 1e-6
CHUNK = 128
SGU_GROUPS = 8
CONV_WIDTH = 3
SUBLANES = 8
LANES = 128
BF16_SUBLANES = 16
TILE_ROWS = 512
MIXER_TILE_ROWS = 1024
HEAD_CHUNK_ROWS = 256
VMEM_LIMIT_BYTES = 56 * 1024 * 1024

B_GATE, C_GATE, XIN, U, V, G_A, G_B = range(7)

_BF16 = jnp.bfloat16
_F32 = jnp.float32


def _rmsnorm(x, g):
    ms = jnp.mean(x * x, axis=-1, keepdims=True)
    return x * lax.rsqrt(ms + EPS) * g


def _dot(a, b):
    return jnp.dot(a, b, preferred_element_type=_F32)


def _store_slabs(zbuf_ref, first_slab, row0, rows, value):
    for k in range(value.shape[1] // LANES):
        zbuf_ref[first_slab + k, row0:row0 + rows, :] = value[:, k * LANES:(k + 1) * LANES]


def _conv3_offset(zbuf_ref, w_ref, rows, slab):
    w = w_ref[:, slab * LANES:(slab + 1) * LANES]
    z2 = zbuf_ref[slab, SUBLANES - 2:SUBLANES - 2 + rows, :]
    z1 = zbuf_ref[slab, SUBLANES - 1:SUBLANES - 1 + rows, :]
    z0 = zbuf_ref[slab, SUBLANES:SUBLANES + rows, :]
    return z2 * w[0:1, :] + z1 * w[1:2, :] + z0 * w[2:3, :]


def _mixer_kernel(x_ref, g_ref, win_ref, cw_ref, lng_ref, lnb_ref, ws_ref, bias_ref, wout_ref,
                  o_ref, h_ref, zbuf_ref, vn_ref, mixed_ref, merged_ref):
    rows, d = x_ref.shape
    first_tile = pl.program_id(1) == 0
    col = lambda k: slice(k * d, (k + 1) * d)

    h_ref[...] = _rmsnorm(x_ref[...], g_ref[...]).astype(_BF16)
    proj = lambda k: _dot(h_ref[...], win_ref[:, col(k)])

    v = jnp.concatenate([_dot(h_ref[r:r + HEAD_CHUNK_ROWS, :], win_ref[:, col(V)])
                         for r in range(0, rows, HEAD_CHUNK_ROWS)], axis=0)
    mu = jnp.mean(v, axis=-1, keepdims=True)
    vc = v - mu
    var = jnp.mean(vc * vc, axis=-1, keepdims=True)
    vn_ref[...] = (vc * lax.rsqrt(var + EPS) * lng_ref[...] + lnb_ref[...]).astype(_BF16)

    prev = zbuf_ref[:, rows:rows + SUBLANES, :]
    zbuf_ref[:, 0:SUBLANES, :] = jnp.where(first_tile, jnp.zeros_like(prev), prev)
    _store_slabs(zbuf_ref, 0, SUBLANES, rows, proj(C_GATE) * proj(XIN))
    conv = jnp.concatenate([_conv3_offset(zbuf_ref, cw_ref, rows, s) for s in range(d // LANES)],
                           axis=1)

    t_idx = lax.broadcasted_iota(jnp.int32, (CHUNK, CHUNK), 0)
    s_idx = lax.broadcasted_iota(jnp.int32, (CHUNK, CHUNK), 1)
    causal = s_idx <= t_idx
    gdim = d // SGU_GROUPS
    n_chunks = rows // CHUNK
    for g in range(SGU_GROUPS):
        wg = jnp.where(causal, ws_ref[g], 0.0).astype(_BF16)
        gcols = slice(g * gdim, (g + 1) * gdim)
        for n in range(0, n_chunks, 2):
            r0 = slice(n * CHUNK, (n + 1) * CHUNK)
            r1 = slice((n + 1) * CHUNK, (n + 2) * CHUNK)
            rhs = jnp.concatenate([vn_ref[r0, gcols], vn_ref[r1, gcols]], axis=1)
            res = _dot(wg, rhs)
            mixed_ref[r0, gcols] = res[:, :gdim]
            mixed_ref[r1, gcols] = res[:, gdim:]
    mixed = mixed_ref[...].reshape(n_chunks, CHUNK, d) + bias_ref[...][None]
    y_b = proj(U) * mixed.reshape(rows, d)
    merged_ref[...] = jax.nn.sigmoid(proj(G_B)) * y_b

    gated_conv = jax.nn.sigmoid(proj(G_A)) * conv
    merged = merged_ref[...] + proj(B_GATE) * gated_conv

    o_ref[...] = x_ref[...] + _dot(merged.astype(_BF16), wout_ref[...])


HIST = 2 * SUBLANES


def _tile_copies(hbm_ref, buf_ref, sem_ref, tile, slot, to_hbm):
    r = buf_ref.shape[1]
    copies = []
    for sub in range(SUBLANES):
        hbm = hbm_ref.at[pl.ds((tile * SUBLANES + sub) * r, r), :]
        vmem = buf_ref.at[slot, :, sub, :]
        src, dst = (vmem, hbm) if to_hbm else (hbm, vmem)
        copies.append(pltpu.make_async_copy(src, dst, sem_ref.at[slot]))
    return copies


def _conv3_groups(zbuf_ref, w_ref, rows, slab):
    w = w_ref[:, slab * LANES:(slab + 1) * LANES]
    z = zbuf_ref[slab]
    z2, z1, z0 = z[0:rows], z[SUBLANES:SUBLANES + rows], z[HIST:HIST + rows]
    return z2 * w[0:1, :] + z1 * w[1:2, :] + z0 * w[2:3, :]


def _fill_history(zbuf_ref, prev, rows, slab):
    last = zbuf_ref[slab, rows:rows + HIST, :]
    is_last_sublane = lax.broadcasted_iota(jnp.int32, (SUBLANES, LANES), 0) == SUBLANES - 1
    for k in range(2):
        rk = slice(k * SUBLANES, (k + 1) * SUBLANES)
        src = jnp.where(is_last_sublane, prev[rk], last[rk])
        zbuf_ref[slab, rk, :] = pltpu.roll(src, 1, 0)


def _ffn_kernel(x_hbm, g_ref, wup_ref, cw_ref, wdown_ref, fg_ref, o_hbm,
                xbuf, obuf, in_sem, out_sem, h_ref, zbuf_ref, act_ref,
                *, final_norm, col_block, tiles_per_seq):
    n_r, _, d = xbuf.shape[1:]
    rows = n_r * SUBLANES
    d_ff = wdown_ref.shape[0]
    slabs_per_block = col_block // LANES
    t = pl.program_id(0)
    n_tiles = pl.num_programs(0)
    slot = lax.rem(t, 2)
    starts_seq = lax.rem(t, tiles_per_seq) == 0

    @pl.when(t == 0)
    def _():
        zbuf_ref[:, rows:rows + HIST, :] = jnp.zeros((zbuf_ref.shape[0], HIST, LANES), _F32)
        for c in _tile_copies(x_hbm, xbuf, in_sem, 0, 0, to_hbm=False):
            c.start()

    @pl.when(t + 1 < n_tiles)
    def _():
        for c in _tile_copies(x_hbm, xbuf, in_sem, t + 1, 1 - slot, to_hbm=False):
            c.start()

    for c in _tile_copies(x_hbm, xbuf, in_sem, t, slot, to_hbm=False):
        c.wait()

    @pl.when(t >= 2)
    def _():
        for c in _tile_copies(o_hbm, obuf, out_sem, t - 2, slot, to_hbm=True):
            c.wait()

    x = xbuf[slot].reshape(rows, d)
    h_ref[...] = _rmsnorm(x, g_ref[...]).astype(_BF16)
    for j in range(d_ff // col_block):
        for c0 in (j * col_block, d_ff + j * col_block):
            up = _dot(h_ref[...], wup_ref[:, c0:c0 + col_block])
            slabs = range(c0 // LANES, c0 // LANES + slabs_per_block)
            prev = [zbuf_ref[s, rows:rows + HIST, :] for s in slabs]
            _store_slabs(zbuf_ref, c0 // LANES, HIST, rows, up)
            for s, p in zip(slabs, prev):
                _fill_history(zbuf_ref, jnp.where(starts_seq, jnp.zeros_like(p), p), rows, s)
        for k in range(slabs_per_block):
            s = j * slabs_per_block + k
            gate = _conv3_groups(zbuf_ref, cw_ref, rows, s)
            val = _conv3_groups(zbuf_ref, cw_ref, rows, d_ff // LANES + s)
            gate, val = gate.astype(_BF16), val.astype(_BF16)
            act_ref[:, s * LANES:(s + 1) * LANES] = gate * jax.nn.sigmoid(gate) * val
    y = x + _dot(act_ref[...], wdown_ref[...])
    if final_norm:
        y = _rmsnorm(y, fg_ref[...])
    obuf[slot] = y.reshape(n_r, SUBLANES, d)

    for c in _tile_copies(o_hbm, obuf, out_sem, t, slot, to_hbm=True):
        c.start()

    @pl.when(t == n_tiles - 1)
    def _():
        for c in _tile_copies(o_hbm, obuf, out_sem, t, slot, to_hbm=True):
            c.wait()

    @pl.when(jnp.logical_and(t == n_tiles - 1, t >= 1))
    def _():
        for c in _tile_copies(o_hbm, obuf, out_sem, t - 1, 1 - slot, to_hbm=True):
            c.wait()


def _layer(stacked, layer):
    index = (layer,) + (0,) * (stacked.ndim - 1)
    return pl.BlockSpec((None,) + stacked.shape[1:], lambda *_: index, pipeline_mode=pl.Buffered(1))


def _whole(array):
    index = (0,) * array.ndim
    return pl.BlockSpec(array.shape, lambda *_: index, pipeline_mode=pl.Buffered(1))


def _with_casts(body, n_inputs, n_casts):
    def kernel_with_casts(*refs):
        inputs, refs = refs[:n_inputs], refs[n_inputs:]
        cast_src, refs = refs[:n_casts], refs[n_casts:]
        out, refs = refs[0], refs[1:]
        cast_dst, scratch = refs[:n_casts], refs[n_casts:]
        for src, dst in zip(cast_src, cast_dst):
            dst[...] = src[...].astype(_BF16)
        body(*inputs, out, *scratch)
    return kernel_with_casts


def _cast_specs(stacked, layer, n_steps, step_of):
    _, k, n = stacked.shape
    block = next(b for b in range(BF16_SUBLANES, k + 1, BF16_SUBLANES)
                 if k % b == 0 and k // b <= n_steps)
    last = k // block - 1
    src = pl.BlockSpec((None, block, n), lambda *ids: (layer, jnp.minimum(step_of(*ids), last), 0))
    dst = pl.BlockSpec((block, n), lambda *ids: (jnp.minimum(step_of(*ids), last), 0))
    return src, dst, jax.ShapeDtypeStruct((k, n), _BF16)


def _mixer_call(x, layer, g, w_in, conv_w, ln_g, ln_b, w_s, bias, w_out, *, cast, cast_layer):
    bsz, seq, d = x.shape
    rows = MIXER_TILE_ROWS
    n_s = seq // rows
    tile = pl.BlockSpec((None, rows, d), lambda b, s: (b, s, 0))
    stacked = lambda p: _layer(p, layer)
    in_specs = [tile, stacked(g), _whole(w_in), stacked(conv_w), stacked(ln_g), stacked(ln_b),
                stacked(w_s), stacked(bias), _whole(w_out)]
    casts = [_cast_specs(w, cast_layer, bsz * n_s, lambda b, s: b * n_s + s) for w in cast]
    outs = pl.pallas_call(
        _with_casts(_mixer_kernel, len(in_specs), len(casts)),
        grid=(bsz, n_s),
        in_specs=in_specs + [c[0] for c in casts],
        out_specs=[tile] + [c[1] for c in casts],
        out_shape=[jax.ShapeDtypeStruct(x.shape, x.dtype)] + [c[2] for c in casts],
        scratch_shapes=[
            pltpu.VMEM((rows, d), _BF16),
            pltpu.VMEM((d // LANES, SUBLANES + rows, LANES), _F32),
            pltpu.VMEM((rows, d), _BF16),
            pltpu.VMEM((rows, d), _F32),
            pltpu.VMEM((rows, d), _F32),
        ],
        compiler_params=pltpu.CompilerParams(
            dimension_semantics=("arbitrary", "arbitrary"),
            vmem_limit_bytes=VMEM_LIMIT_BYTES),
        name="token_mixer",
    )(x, g, w_in, conv_w, ln_g, ln_b, w_s, bias, w_out, *cast)
    return outs[0], outs[1:]


def _ffn_call(x, layer, g, w_up, conv_w, w_down, final_g, *, final_norm, cast, cast_layer):
    bsz, seq, d = x.shape
    d_ff = w_down.shape[0]
    rows = TILE_ROWS
    n_r = rows // SUBLANES
    n_slabs = 2 * d_ff // LANES
    n_tiles = bsz * seq // rows
    x2 = x.reshape(bsz * seq, d)
    stacked = lambda p: _layer(p, layer)
    in_specs = [pl.BlockSpec(memory_space=pl.ANY), stacked(g), _whole(w_up), stacked(conv_w),
                _whole(w_down), _layer(final_g, 0)]
    casts = [_cast_specs(w, cast_layer, n_tiles, lambda t: t) for w in cast]
    body = functools.partial(_ffn_kernel, final_norm=final_norm, col_block=256,
                             tiles_per_seq=seq // rows)
    outs = pl.pallas_call(
        _with_casts(body, len(in_specs), len(casts)),
        grid=(n_tiles,),
        in_specs=in_specs + [c[0] for c in casts],
        out_specs=[pl.BlockSpec(memory_space=pl.ANY)] + [c[1] for c in casts],
        out_shape=[jax.ShapeDtypeStruct(x2.shape, x.dtype)] + [c[2] for c in casts],
        scratch_shapes=[
            pltpu.VMEM((2, n_r, SUBLANES, d), _F32),
            pltpu.VMEM((2, n_r, SUBLANES, d), _F32),
            pltpu.SemaphoreType.DMA((2,)),
            pltpu.SemaphoreType.DMA((2,)),
            pltpu.VMEM((rows, d), _BF16),
            pltpu.VMEM((n_slabs, HIST + rows, LANES), _F32),
            pltpu.VMEM((rows, d_ff), _BF16),
        ],
        compiler_params=pltpu.CompilerParams(
            dimension_semantics=("arbitrary",),
            vmem_limit_bytes=VMEM_LIMIT_BYTES),
        name="channel_mixer",
    )(x2, g, w_up, conv_w, w_down, final_g, *cast)
    return outs[0].reshape(x.shape), outs[1:]


def kernel(x, mix_norm_g, w_in, conv_a_w, ln_v_g, ln_v_b, w_s, b_s, w_out, ffn_norm_g, w_up,
           conv_ffn_w, w_down, final_norm_g):
    depth, d = mix_norm_g.shape
    assert x.shape[1] % TILE_ROWS == 0 and x.shape[1] % MIXER_TILE_ROWS == 0
    assert MIXER_TILE_ROWS % (2 * CHUNK) == 0
    gdim = d // SGU_GROUPS
    rows_of = lambda a: a.reshape(a.shape[0], 1, a.shape[1])
    bias = jnp.repeat(jnp.swapaxes(b_s, 1, 2), gdim, axis=-1)
    final_g = final_norm_g.reshape(1, 1, d)
    wb_in, wb_out = w_in[0].astype(_BF16), w_out[0].astype(_BF16)
    for l in range(depth):
        x, (wb_up, wb_down) = _mixer_call(
            x, l, rows_of(mix_norm_g), wb_in, conv_a_w, rows_of(ln_v_g), rows_of(ln_v_b), w_s, bias,
            wb_out, cast=(w_up, w_down), cast_layer=l)
        last = l == depth - 1
        x, next_weights = _ffn_call(
            x, l, rows_of(ffn_norm_g), wb_up, conv_ffn_w, wb_down, final_g, final_norm=last,
            cast=() if last else (w_in, w_out), cast_layer=l + 1)
        if not last:
            wb_in, wb_out = next_weights
    return x
```

```python
import functools

import jax
import jax.numpy as jnp
from jax import lax
from jax.experimental import pallas as pl
from jax.experimental.pallas import tpu as pltpu

EPS = 1e-6
CHUNK = 128
SGU_GROUPS = 8
CONV_WIDTH = 3
SUBLANES = 8
LANES = 128
BF16_SUBLANES = 16
TILE_ROWS = 512
MIXER_TILE_ROWS = 1024
HEAD_CHUNK_ROWS = 256
VMEM_LIMIT_BYTES = 56 * 1024 * 1024

B_GATE, C_GATE, XIN, U, V, G_A, G_B = range(7)

_BF16 = jnp.bfloat16
_F32 = jnp.float32


def _rmsnorm(x, g):
    ms = jnp.mean(x * x, axis=-1, keepdims=True)
    return x * lax.rsqrt(ms + EPS) * g


def _dot(a, b):
    return jnp.dot(a, b, preferred_element_type=_F32)


def _store_slabs(zbuf_ref, first_slab, row0, rows, value):
    for k in range(value.shape[1] // LANES):
        zbuf_ref[first_slab + k, row0:row0 + rows, :] = value[:, k * LANES:(k + 1) * LANES]


def _conv3_offset(zbuf_ref, w_ref, rows, slab):
    w = w_ref[:, slab * LANES:(slab + 1) * LANES]
    z2 = zbuf_ref[slab, SUBLANES - 2:SUBLANES - 2 + rows, :]
    z1 = zbuf_ref[slab, SUBLANES - 1:SUBLANES - 1 + rows, :]
    z0 = zbuf_ref[slab, SUBLANES:SUBLANES + rows, :]
    return z2 * w[0:1, :] + z1 * w[1:2, :] + z0 * w[2:3, :]


def _mixer_kernel(x_ref, g_ref, win_ref, cw_ref, lng_ref, lnb_ref, ws_ref, bias_ref, wout_ref,
                  o_ref, h_ref, zbuf_ref, vn_ref, mixed_ref, merged_ref):
    rows, d = x_ref.shape
    first_tile = pl.program_id(1) == 0
    col = lambda k: slice(k * d, (k + 1) * d)

    h_ref[...] = _rmsnorm(x_ref[...], g_ref[...]).astype(_BF16)
    proj = lambda k: _dot(h_ref[...], win_ref[:, col(k)])

    v = jnp.concatenate([_dot(h_ref[r:r + HEAD_CHUNK_ROWS, :], win_ref[:, col(V)])
                         for r in range(0, rows, HEAD_CHUNK_ROWS)], axis=0)
    mu = jnp.mean(v, axis=-1, keepdims=True)
    vc = v - mu
    var = jnp.mean(vc * vc, axis=-1, keepdims=True)
    vn_ref[...] = (vc * lax.rsqrt(var + EPS) * lng_ref[...] + lnb_ref[...]).astype(_BF16)

    prev = zbuf_ref[:, rows:rows + SUBLANES, :]
    zbuf_ref[:, 0:SUBLANES, :] = jnp.where(first_tile, jnp.zeros_like(prev), prev)
    _store_slabs(zbuf_ref, 0, SUBLANES, rows, proj(C_GATE) * proj(XIN))
    conv = jnp.concatenate([_conv3_offset(zbuf_ref, cw_ref, rows, s) for s in range(d // LANES)],
                           axis=1)

    t_idx = lax.broadcasted_iota(jnp.int32, (CHUNK, CHUNK), 0)
    s_idx = lax.broadcasted_iota(jnp.int32, (CHUNK, CHUNK), 1)
    causal = s_idx <= t_idx
    gdim = d // SGU_GROUPS
    n_chunks = rows // CHUNK
    for g in range(SGU_GROUPS):
        wg = jnp.where(causal, ws_ref[g], 0.0).astype(_BF16)
        gcols = slice(g * gdim, (g + 1) * gdim)
        for n in range(0, n_chunks, 2):
            r0 = slice(n * CHUNK, (n + 1) * CHUNK)
            r1 = slice((n + 1) * CHUNK, (n + 2) * CHUNK)
            rhs = jnp.concatenate([vn_ref[r0, gcols], vn_ref[r1, gcols]], axis=1)
            res = _dot(wg, rhs)
            mixed_ref[r0, gcols] = res[:, :gdim]
            mixed_ref[r1, gcols] = res[:, gdim:]
    mixed = mixed_ref[...].reshape(n_chunks, CHUNK, d) + bias_ref[...][None]
    y_b = proj(U) * mixed.reshape(rows, d)
    merged_ref[...] = jax.nn.sigmoid(proj(G_B)) * y_b

    gated_conv = jax.nn.sigmoid(proj(G_A)) * conv
    merged = merged_ref[...] + proj(B_GATE) * gated_conv

    o_ref[...] = x_ref[...] + _dot(merged.astype(_BF16), wout_ref[...])


HIST = 2 * SUBLANES


def _tile_copies(hbm_ref, buf_ref, sem_ref, tile, slot, to_hbm):
    r = buf_ref.shape[1]
    copies = []
    for sub in range(SUBLANES):
        hbm = hbm_ref.at[pl.ds((tile * SUBLANES + sub) * r, r), :]
        vmem = buf_ref.at[slot, :, sub, :]
        src, dst = (vmem, hbm) if to_hbm else (hbm, vmem)
        copies.append(pltpu.make_async_copy(src, dst, sem_ref.at[slot]))
    return copies


def _start_all(copies):
    for i, c in enumerate(copies):
        c.start(priority=i % 2)


def _conv3_groups(zbuf_ref, w_ref, rows, slab):
    w = w_ref[:, slab * LANES:(slab + 1) * LANES]
    z = zbuf_ref[slab]
    z2, z1, z0 = z[0:rows], z[SUBLANES:SUBLANES + rows], z[HIST:HIST + rows]
    return z2 * w[0:1, :] + z1 * w[1:2, :] + z0 * w[2:3, :]


def _fill_history(zbuf_ref, prev, rows, slab):
    last = zbuf_ref[slab, rows:rows + HIST, :]
    is_last_sublane = lax.broadcasted_iota(jnp.int32, (SUBLANES, LANES), 0) == SUBLANES - 1
    for k in range(2):
        rk = slice(k * SUBLANES, (k + 1) * SUBLANES)
        src = jnp.where(is_last_sublane, prev[rk], last[rk])
        zbuf_ref[slab, rk, :] = pltpu.roll(src, 1, 0)


def _ffn_kernel(x_hbm, g_ref, wup_ref, cw_ref, wdown_ref, fg_ref, o_hbm,
                xbuf, obuf, in_sem, out_sem, h_ref, zbuf_ref, act_ref,
                *, final_norm, col_block, tiles_per_seq):
    n_r, _, d = xbuf.shape[1:]
    rows = n_r * SUBLANES
    d_ff = wdown_ref.shape[0]
    slabs_per_block = col_block // LANES
    t = pl.program_id(0)
    n_tiles = pl.num_programs(0)
    slot = lax.rem(t, 2)
    starts_seq = lax.rem(t, tiles_per_seq) == 0

    @pl.when(t == 0)
    def _():
        zbuf_ref[:, rows:rows + HIST, :] = jnp.zeros((zbuf_ref.shape[0], HIST, LANES), _F32)
        _start_all(_tile_copies(x_hbm, xbuf, in_sem, 0, 0, to_hbm=False))

    @pl.when(t + 1 < n_tiles)
    def _():
        _start_all(_tile_copies(x_hbm, xbuf, in_sem, t + 1, 1 - slot, to_hbm=False))

    for c in _tile_copies(x_hbm, xbuf, in_sem, t, slot, to_hbm=False):
        c.wait()

    @pl.when(t >= 2)
    def _():
        for c in _tile_copies(o_hbm, obuf, out_sem, t - 2, slot, to_hbm=True):
            c.wait()

    x = xbuf[slot].reshape(rows, d)
    h_ref[...] = _rmsnorm(x, g_ref[...]).astype(_BF16)
    for j in range(d_ff // col_block):
        for c0 in (j * col_block, d_ff + j * col_block):
            up = _dot(h_ref[...], wup_ref[:, c0:c0 + col_block])
            slabs = range(c0 // LANES, c0 // LANES + slabs_per_block)
            prev = [zbuf_ref[s, rows:rows + HIST, :] for s in slabs]
            _store_slabs(zbuf_ref, c0 // LANES, HIST, rows, up)
            for s, p in zip(slabs, prev):
                _fill_history(zbuf_ref, jnp.where(starts_seq, jnp.zeros_like(p), p), rows, s)
        for k in range(slabs_per_block):
            s = j * slabs_per_block + k
            gate = _conv3_groups(zbuf_ref, cw_ref, rows, s)
            val = _conv3_groups(zbuf_ref, cw_ref, rows, d_ff // LANES + s)
            act_ref[:, s * LANES:(s + 1) * LANES] = (gate * jax.nn.sigmoid(gate) * val).astype(_BF16)
    y = x + _dot(act_ref[...], wdown_ref[...])
    if final_norm:
        y = _rmsnorm(y, fg_ref[...])
    obuf[slot] = y.reshape(n_r, SUBLANES, d)

    _start_all(_tile_copies(o_hbm, obuf, out_sem, t, slot, to_hbm=True))

    @pl.when(t == n_tiles - 1)
    def _():
        for c in _tile_copies(o_hbm, obuf, out_sem, t, slot, to_hbm=True):
            c.wait()

    @pl.when(jnp.logical_and(t == n_tiles - 1, t >= 1))
    def _():
        for c in _tile_copies(o_hbm, obuf, out_sem, t - 1, 1 - slot, to_hbm=True):
            c.wait()


def _layer(stacked, layer):
    index = (layer,) + (0,) * (stacked.ndim - 1)
    return pl.BlockSpec((None,) + stacked.shape[1:], lambda *_: index, pipeline_mode=pl.Buffered(1))


def _whole(array):
    index = (0,) * array.ndim
    return pl.BlockSpec(array.shape, lambda *_: index, pipeline_mode=pl.Buffered(1))


def _with_casts(body, n_inputs, n_casts):
    def kernel_with_casts(*refs):
        inputs, refs = refs[:n_inputs], refs[n_inputs:]
        cast_src, refs = refs[:n_casts], refs[n_casts:]
        out, refs = refs[0], refs[1:]
        cast_dst, scratch = refs[:n_casts], refs[n_casts:]
        for src, dst in zip(cast_src, cast_dst):
            dst[...] = src[...].astype(_BF16)
        body(*inputs, out, *scratch)
    return kernel_with_casts


def _cast_specs(stacked, layer, n_steps, step_of):
    _, k, n = stacked.shape
    block = next(b for b in range(BF16_SUBLANES, k + 1, BF16_SUBLANES)
                 if k % b == 0 and k // b <= n_steps)
    last = k // block - 1
    src = pl.BlockSpec((None, block, n), lambda *ids: (layer, jnp.minimum(step_of(*ids), last), 0))
    dst = pl.BlockSpec((block, n), lambda *ids: (jnp.minimum(step_of(*ids), last), 0))
    return src, dst, jax.ShapeDtypeStruct((k, n), _BF16)


def _mixer_call(x, layer, g, w_in, conv_w, ln_g, ln_b, w_s, bias, w_out, *, cast, cast_layer):
    bsz, seq, d = x.shape
    rows = MIXER_TILE_ROWS
    n_s = seq // rows
    tile = pl.BlockSpec((None, rows, d), lambda b, s: (b, s, 0))
    stacked = lambda p: _layer(p, layer)
    in_specs = [tile, stacked(g), _whole(w_in), stacked(conv_w), stacked(ln_g), stacked(ln_b),
                stacked(w_s), stacked(bias), _whole(w_out)]
    casts = [_cast_specs(w, cast_layer, bsz * n_s, lambda b, s: b * n_s + s) for w in cast]
    outs = pl.pallas_call(
        _with_casts(_mixer_kernel, len(in_specs), len(casts)),
        grid=(bsz, n_s),
        in_specs=in_specs + [c[0] for c in casts],
        out_specs=[tile] + [c[1] for c in casts],
        out_shape=[jax.ShapeDtypeStruct(x.shape, x.dtype)] + [c[2] for c in casts],
        scratch_shapes=[
            pltpu.VMEM((rows, d), _BF16),
            pltpu.VMEM((d // LANES, SUBLANES + rows, LANES), _F32),
            pltpu.VMEM((rows, d), _BF16),
            pltpu.VMEM((rows, d), _F32),
            pltpu.VMEM((rows, d), _F32),
        ],
        compiler_params=pltpu.CompilerParams(
            dimension_semantics=("arbitrary", "arbitrary"),
            vmem_limit_bytes=VMEM_LIMIT_BYTES),
        name="token_mixer",
    )(x, g, w_in, conv_w, ln_g, ln_b, w_s, bias, w_out, *cast)
    return outs[0], outs[1:]


def _ffn_call(x, layer, g, w_up, conv_w, w_down, final_g, *, final_norm, cast, cast_layer):
    bsz, seq, d = x.shape
    d_ff = w_down.shape[0]
    rows = TILE_ROWS
    n_r = rows // SUBLANES
    n_slabs = 2 * d_ff // LANES
    n_tiles = bsz * seq // rows
    x2 = x.reshape(bsz * seq, d)
    stacked = lambda p: _layer(p, layer)
    in_specs = [pl.BlockSpec(memory_space=pl.ANY), stacked(g), _whole(w_up), stacked(conv_w),
                _whole(w_down), _layer(final_g, 0)]
    casts = [_cast_specs(w, cast_layer, n_tiles, lambda t: t) for w in cast]
    body = functools.partial(_ffn_kernel, final_norm=final_norm, col_block=256,
                             tiles_per_seq=seq // rows)
    outs = pl.pallas_call(
        _with_casts(body, len(in_specs), len(casts)),
        grid=(n_tiles,),
        in_specs=in_specs + [c[0] for c in casts],
        out_specs=[pl.BlockSpec(memory_space=pl.ANY)] + [c[1] for c in casts],
        out_shape=[jax.ShapeDtypeStruct(x2.shape, x.dtype)] + [c[2] for c in casts],
        scratch_shapes=[
            pltpu.VMEM((2, n_r, SUBLANES, d), _F32),
            pltpu.VMEM((2, n_r, SUBLANES, d), _F32),
            pltpu.SemaphoreType.DMA((2,)),
            pltpu.SemaphoreType.DMA((2,)),
            pltpu.VMEM((rows, d), _BF16),
            pltpu.VMEM((n_slabs, HIST + rows, LANES), _F32),
            pltpu.VMEM((rows, d_ff), _BF16),
        ],
        compiler_params=pltpu.CompilerParams(
            dimension_semantics=("arbitrary",),
            vmem_limit_bytes=VMEM_LIMIT_BYTES),
        name="channel_mixer",
    )(x2, g, w_up, conv_w, w_down, final_g, *cast)
    return outs[0].reshape(x.shape), outs[1:]


def kernel(x, mix_norm_g, w_in, conv_a_w, ln_v_g, ln_v_b, w_s, b_s, w_out, ffn_norm_g, w_up,
           conv_ffn_w, w_down, final_norm_g):
    depth, d = mix_norm_g.shape
    assert x.shape[1] % TILE_ROWS == 0 and x.shape[1] % MIXER_TILE_ROWS == 0
    assert MIXER_TILE_ROWS % (2 * CHUNK) == 0
    gdim = d // SGU_GROUPS
    rows_of = lambda a: a.reshape(a.shape[0], 1, a.shape[1])
    bias = jnp.repeat(jnp.swapaxes(b_s, 1, 2), gdim, axis=-1)
    final_g = final_norm_g.reshape(1, 1, d)
    wb_in, wb_out = w_in[0].astype(_BF16), w_out[0].astype(_BF16)
    for l in range(depth):
        x, (wb_up, wb_down) = _mixer_call(
            x, l, rows_of(mix_norm_g), wb_in, conv_a_w, rows_of(ln_v_g), rows_of(ln_v_b), w_s, bias,
            wb_out, cast=(w_up, w_down), cast_layer=l)
        last = l == depth - 1
        x, next_weights = _ffn_call(
            x, l, rows_of(ffn_norm_g), wb_up, conv_ffn_w, wb_down, final_g, final_norm=last,
            cast=() if last else (w_in, w_out), cast_layer=l + 1)
        if not last:
            wb_in, wb_out = next_weights
    return x
```
